```python
import jax, jax.numpy as jnp
from jax import lax
import numpy as np

D_MODEL = 1024
BATCH = 2
SEQ = 8192
DEPTH = 2
DEC_BATCH = 16
DEC_SEQ = 64
PAST_LEN = 4096

CHUNK = 64
N_A_LAYERS = (DEPTH + 1) // 2
N_C_LAYERS = DEPTH // 2

RWKV_WIDTH = D_MODEL // 2
RWKV_HEAD = 64
RWKV_HEADS = RWKV_WIDTH // RWKV_HEAD
DECAY_LORA = 64
ICLR_LORA = 64
GATE_LORA = 128
LN_X_EPS = 64e-5
POOL_WIDTH = D_MODEL - RWKV_WIDTH
POOL_WINDOWS = (2, 4, 8, 16)
POOL_GROUPS = len(POOL_WINDOWS)
POOL_GROUP_DIM = POOL_WIDTH // POOL_GROUPS
POOL_HIST = max(POOL_WINDOWS) - 1
SHIFT_WIDTH = 3 * RWKV_WIDTH + DECAY_LORA + ICLR_LORA + GATE_LORA
SHIFT_SPLITS = (RWKV_WIDTH, 2 * RWKV_WIDTH, 3 * RWKV_WIDTH, 3 * RWKV_WIDTH + DECAY_LORA,
                3 * RWKV_WIDTH + DECAY_LORA + ICLR_LORA)
IN_A_WIDTH = SHIFT_WIDTH + POOL_WIDTH
SB_HEAD = 64
SB_HEADS = D_MODEL // SB_HEAD
Q_BLOCK = 128
N_EXPERTS = 32
TOP_K = 4
D_EXPERT = D_MODEL
SWIGLU_LIMIT = 7.0
SWIGLU_ALPHA = 1.702
ROW_BLOCK = 128
EPS = 1e-6

kernel_name = 'rwkv7_pool_stickbreak_moe_stream_step'


def rms_norm(x, g):
    xf = x.astype(jnp.float32)
    y = xf * lax.rsqrt(jnp.mean(xf * xf, axis=-1, keepdims=True) + EPS)
    return (y * g.astype(jnp.float32)).astype(x.dtype)


def modulate(x, g, shift, scale):
    return rms_norm(x, g) * (1 + scale[:, None, :]) + shift[:, None, :]


def wkv_scan(s0, r, w, k, v, a_vec, b_vec):
    def step(s, inp):
        r_t, w_t, k_t, v_t, a_t, b_t = inp
        sa = jnp.einsum('bhvk,bhk->bhv', s, a_t)
        s = s * w_t[:, :, None, :] + sa[..., None] * b_t[:, :, None, :] + v_t[..., None] * k_t[:, :, None, :]
        return s, jnp.einsum('bhvk,bhk->bhv', s, r_t)
    seq = tuple(jnp.moveaxis(t, 1, 0) for t in (r, w, k, v, a_vec, b_vec))
    s, y = lax.scan(step, s0, seq)
    return s, jnp.moveaxis(y, 0, 1)


def rwkv_pool_mixer(h, pos0, wkv0, shift0, pool0, w_in, mu, w0, w_decay, a0, w_iclr, w_gate,
                    k_k, k_a, r_k, ln_w, ln_b, w_pool, pool_scale, w_out):
    f32 = jnp.float32
    B, T, _ = h.shape
    p = h @ w_in
    ps, pp = p[..., :SHIFT_WIDTH], p[..., SHIFT_WIDTH:]
    prev = jnp.concatenate([shift0[:, None, :].astype(ps.dtype), ps[:, :-1]], axis=1)
    xs = ps + (prev - ps) * mu
    r, k, v, xw, xa, xg = jnp.split(xs, SHIFT_SPLITS, axis=-1)
    log_w = -jax.nn.softplus(-(w0 + jnp.tanh(xw) @ w_decay).astype(f32)) - 0.5
    decay = jnp.exp(-jnp.exp(log_w))
    a = jax.nn.sigmoid((a0 + xa @ w_iclr).astype(f32))
    g = jax.nn.sigmoid(xg) @ w_gate

    def heads(t):
        return t.astype(f32).reshape(B, T, RWKV_HEADS, RWKV_HEAD)
    kk = heads(k * k_k)
    kk = kk * lax.rsqrt(jnp.maximum(jnp.sum(kk * kk, axis=-1, keepdims=True), 1e-24))
    k_mod = k.astype(f32) * (1 + (a - 1) * k_a.astype(f32))
    rh, kh, vh, ah, wh = heads(r), heads(k_mod), heads(v), heads(a), heads(decay)
    s_new, y = wkv_scan(wkv0.astype(f32), rh, wh, kh, vh, -kk, kk * ah)
    mean = jnp.mean(y, axis=-1, keepdims=True)
    var = jnp.mean(jnp.square(y - mean), axis=-1, keepdims=True)
    y = ((y - mean) * lax.rsqrt(var + LN_X_EPS)).reshape(B, T, RWKV_WIDTH) * ln_w + ln_b
    bonus = jnp.sum(rh * kh * r_k.astype(f32), axis=-1, keepdims=True) * vh
    y_rwkv = ((y + bonus.reshape(B, T, RWKV_WIDTH)) * g).astype(h.dtype)

    u = jnp.concatenate([pool0.astype(pp.dtype), pp], axis=1)
    uf = u.astype(f32)
    cs = jnp.concatenate([jnp.zeros((B, 1, POOL_WIDTH), f32), jnp.cumsum(uf, axis=1)], axis=1)
    pos = pos0 + jnp.arange(T)
    means = []
    for gi, win in enumerate(POOL_WINDOWS):
        lo, hi = gi * POOL_GROUP_DIM, (gi + 1) * POOL_GROUP_DIM
        s = cs[:, POOL_HIST + 1:POOL_HIST + 1 + T, lo:hi] - cs[:, POOL_HIST + 1 - win:POOL_HIST + 1 - win + T, lo:hi]
        cnt = jnp.minimum(pos + 1, win).astype(f32)
        means.append(s / cnt[None, :, None])
    d = (jnp.concatenate(means, axis=-1) - pp.astype(f32)).astype(h.dtype)
    d = d.reshape(B, T, POOL_GROUPS, POOL_GROUP_DIM)
    y_pool = jnp.einsum('btgc,gcd->btgd', d, w_pool).reshape(B, T, POOL_WIDTH) * pool_scale

    out = jnp.concatenate([y_rwkv, y_pool.astype(h.dtype)], axis=-1) @ w_out
    return out, s_new, ps[:, -1], u[:, -POOL_HIST:]


def stick_breaking(q, k, v, q_pos0):
    f32 = jnp.float32
    Tq = q.shape[1]
    scale = SB_HEAD ** -0.5
    outs = []
    for b0 in range(0, Tq, Q_BLOCK):
        b1 = min(b0 + Q_BLOCK, Tq)
        kend = q_pos0 + b1 - 1
        kb, vb = k[:, :kend], v[:, :kend]
        logits = jnp.einsum('bqhd,bkhd->bhqk', q[:, b0:b1], kb).astype(f32) * scale
        qpos = q_pos0 + jnp.arange(b0, b1)
        mask = jnp.arange(kend)[None, :] < qpos[:, None]
        log_1m = jnp.where(mask, jax.nn.log_sigmoid(-logits), 0.0)
        after = lax.cumsum(log_1m, axis=3, reverse=True) - log_1m
        wts = jnp.where(mask, jnp.exp(jax.nn.log_sigmoid(logits) + after), 0.0)
        outs.append(jnp.einsum('bhqk,bkhd->bqhd', wts, vb.astype(f32)))
    return jnp.concatenate(outs, axis=1).astype(q.dtype)


def stick_breaking_mixer(h, pos0, ck, cv, w_qkv, w_out):
    B, T, _ = h.shape
    q, k, v = jnp.split(h @ w_qkv, 3, axis=-1)
    q = q.reshape(B, T, SB_HEADS, SB_HEAD)
    k = k.reshape(B, T, SB_HEADS, SB_HEAD)
    v = v.reshape(B, T, SB_HEADS, SB_HEAD)
    k_all = jnp.concatenate([ck.astype(k.dtype), k], axis=1)
    v_all = jnp.concatenate([cv.astype(v.dtype), v], axis=1)
    o = stick_breaking(q, k_all, v_all, pos0)
    return o.reshape(B, T, D_MODEL) @ w_out, k, v


def moe_ffn(h, w_router, b_router, w_up, b_up, w_down, b_down):
    B, T, D = h.shape
    n = B * T
    xt = h.reshape(n, D)
    logits = (xt @ w_router).astype(jnp.float32) + b_router.astype(jnp.float32)
    top_vals, top_idx = lax.top_k(logits, TOP_K)
    gates = jax.nn.softmax(top_vals, axis=-1)
    flat_e = top_idx.reshape(-1)
    flat_tok = jnp.repeat(jnp.arange(n, dtype=jnp.int32), TOP_K)
    flat_gate = gates.reshape(-1)
    order = jnp.argsort(flat_e)
    se, stok, sgate = flat_e[order], flat_tok[order], flat_gate[order]
    counts = jnp.bincount(flat_e, length=N_EXPERTS)
    padded = (counts + ROW_BLOCK - 1) // ROW_BLOCK * ROW_BLOCK
    pad_end = jnp.cumsum(padded)
    pad_start = pad_end - padded
    start = jnp.cumsum(counts) - counts
    dest = pad_start[se] + (jnp.arange(n * TOP_K) - start[se])
    n_blocks = -(-(n * TOP_K + N_EXPERTS * (ROW_BLOCK - 1)) // ROW_BLOCK)
    rows = n_blocks * ROW_BLOCK
    row_tok = jnp.full((rows,), n, jnp.int32).at[dest].set(stok)
    row_gate = jnp.zeros((rows,), jnp.float32).at[dest].set(sgate)
    block_expert = jnp.minimum(
        jnp.searchsorted(pad_end, jnp.arange(n_blocks) * ROW_BLOCK, side='right'), N_EXPERTS - 1)
    x_pad = jnp.concatenate([xt, jnp.zeros((1, D), xt.dtype)], axis=0)
    xb = x_pad[row_tok].reshape(n_blocks, ROW_BLOCK, D)

    def expert_block(args):
        xg, e = args
        u = xg @ w_up[e] + b_up[e]
        glu = jnp.minimum(u[..., ::2], SWIGLU_LIMIT)
        lin = jnp.clip(u[..., 1::2], -SWIGLU_LIMIT, SWIGLU_LIMIT)
        act = glu * jax.nn.sigmoid(SWIGLU_ALPHA * glu) * (lin + 1)
        return act @ w_down[e] + b_down[e]

    yb = lax.map(expert_block, (xb, block_expert)).reshape(rows, D)
    out = jnp.zeros((n + 1, D), jnp.float32).at[row_tok].add(yb.astype(jnp.float32) * row_gate[:, None])
    return out[:n].astype(h.dtype).reshape(B, T, D)


def setup_inputs(seed: int = 0) -> dict:
    key = jax.random.key(seed)
    ks = iter(jax.random.split(key, 48))
    f32 = jnp.float32

    def nrm(shape, s=1.0):
        return jax.random.normal(next(ks), shape, f32) * s

    D = D_MODEL
    inp = {}
    inp['x_prompt'] = nrm((BATCH, SEQ, D))
    inp['x_sample'] = nrm((DEC_BATCH, DEC_SEQ, D))
    inp['c_prompt'] = nrm((BATCH, D))
    inp['c_sample'] = nrm((DEC_BATCH, D))
    inp['state_wkv'] = nrm((N_A_LAYERS, DEC_BATCH, RWKV_HEADS, RWKV_HEAD, RWKV_HEAD), 0.5)
    inp['state_shift'] = nrm((N_A_LAYERS, DEC_BATCH, SHIFT_WIDTH))
    inp['state_pool'] = nrm((N_A_LAYERS, DEC_BATCH, POOL_HIST, POOL_WIDTH))
    inp['cache_k'] = nrm((N_C_LAYERS, DEC_BATCH, PAST_LEN, SB_HEADS, SB_HEAD))
    inp['cache_v'] = nrm((N_C_LAYERS, DEC_BATCH, PAST_LEN, SB_HEADS, SB_HEAD))
    inp['ada_w'] = nrm((DEPTH, D, 6 * D), 0.5 * D ** -0.5)
    inp['ada_b'] = nrm((DEPTH, 6 * D), 0.02)
    inp['norm_mix_g'] = 1.0 + nrm((DEPTH, D), 0.02)
    inp['norm_ffn_g'] = 1.0 + nrm((DEPTH, D), 0.02)
    inp['final_g'] = 1.0 + nrm((D,), 0.02)
    inp['a_w_in'] = nrm((N_A_LAYERS, D, IN_A_WIDTH), D ** -0.5)
    inp['a_mu'] = jax.random.uniform(next(ks), (N_A_LAYERS, SHIFT_WIDTH), f32)
    inp['a_w0'] = -2.0 + nrm((N_A_LAYERS, RWKV_WIDTH), 0.5)
    inp['a_w_decay'] = nrm((N_A_LAYERS, DECAY_LORA, RWKV_WIDTH), 0.1 * DECAY_LORA ** -0.5)
    inp['a_a0'] = nrm((N_A_LAYERS, RWKV_WIDTH), 0.1)
    inp['a_w_iclr'] = nrm((N_A_LAYERS, ICLR_LORA, RWKV_WIDTH), 0.1 * ICLR_LORA ** -0.5)
    inp['a_w_gate'] = nrm((N_A_LAYERS, GATE_LORA, RWKV_WIDTH), GATE_LORA ** -0.5)
    inp['a_k_k'] = 0.85 + nrm((N_A_LAYERS, RWKV_WIDTH), 0.02)
    inp['a_k_a'] = 1.0 + nrm((N_A_LAYERS, RWKV_WIDTH), 0.02)
    inp['a_r_k'] = nrm((N_A_LAYERS, RWKV_HEADS, RWKV_HEAD), 0.1)
    inp['a_ln_w'] = 1.0 + nrm((N_A_LAYERS, RWKV_WIDTH), 0.02)
    inp['a_ln_b'] = nrm((N_A_LAYERS, RWKV_WIDTH), 0.02)
    inp['b_w_pool'] = nrm((N_A_LAYERS, POOL_GROUPS, POOL_GROUP_DIM, POOL_GROUP_DIM), POOL_GROUP_DIM ** -0.5)
    inp['b_pool_scale'] = 1.0 + nrm((N_A_LAYERS, POOL_WIDTH), 0.02)
    inp['ab_w_out'] = nrm((N_A_LAYERS, D, D), D ** -0.5)
    inp['c_w_qkv'] = nrm((N_C_LAYERS, D, 3 * D), D ** -0.5)
    inp['c_w_out'] = nrm((N_C_LAYERS, D, D), D ** -0.5)
    inp['moe_w_router'] = nrm((DEPTH, D, N_EXPERTS), D ** -0.5)
    inp['moe_b_router'] = nrm((DEPTH, N_EXPERTS), 0.01)
    inp['moe_w_up'] = nrm((DEPTH, N_EXPERTS, D, 2 * D_EXPERT), D ** -0.5)
    inp['moe_b_up'] = nrm((DEPTH, N_EXPERTS, 2 * D_EXPERT), 0.01)
    inp['moe_w_down'] = nrm((DEPTH, N_EXPERTS, D_EXPERT, D), D_EXPERT ** -0.5)
    inp['moe_b_down'] = nrm((DEPTH, N_EXPERTS, D), 0.01)
    return inp


def reference(x_prompt, x_sample, c_prompt, c_sample, state_wkv, state_shift, state_pool, cache_k, cache_v,
              ada_w, ada_b, norm_mix_g, norm_ffn_g, final_g,
              a_w_in, a_mu, a_w0, a_w_decay, a_a0, a_w_iclr, a_w_gate, a_k_k, a_k_a, a_r_k, a_ln_w, a_ln_b,
              b_w_pool, b_pool_scale, ab_w_out, c_w_qkv, c_w_out,
              moe_w_router, moe_b_router, moe_w_up, moe_b_up, moe_w_down, moe_b_down):

    def run(x, c, pos0, wkv, shift, pool, ck, cv):
        wkv_o, shift_o, pool_o, k_o, v_o = [], [], [], [], []
        for layer in range(DEPTH):
            i = layer // 2
            mod = jax.nn.silu(c) @ ada_w[layer] + ada_b[layer]
            sh1, sc1, g1, sh2, sc2, g2 = jnp.split(mod, 6, axis=-1)
            h = modulate(x, norm_mix_g[layer], sh1, sc1)
            if layer % 2 == 0:
                mix, s_w, s_s, s_p = rwkv_pool_mixer(
                    h, pos0, wkv[i], shift[i], pool[i], a_w_in[i], a_mu[i], a_w0[i], a_w_decay[i], a_a0[i],
                    a_w_iclr[i], a_w_gate[i], a_k_k[i], a_k_a[i], a_r_k[i], a_ln_w[i], a_ln_b[i],
                    b_w_pool[i], b_pool_scale[i], ab_w_out[i])
                wkv_o.append(s_w)
                shift_o.append(s_s)
                pool_o.append(s_p)
            else:
                mix, k_new, v_new = stick_breaking_mixer(h, pos0, ck[i], cv[i], c_w_qkv[i], c_w_out[i])
                k_o.append(k_new)
                v_o.append(v_new)
            x = x + g1[:, None, :] * mix
            h = modulate(x, norm_ffn_g[layer], sh2, sc2)
            x = x + g2[:, None, :] * moe_ffn(h, moe_w_router[layer], moe_b_router[layer], moe_w_up[layer],
                                             moe_b_up[layer], moe_w_down[layer], moe_b_down[layer])
        return (rms_norm(x, final_g), jnp.stack(wkv_o), jnp.stack(shift_o), jnp.stack(pool_o),
                jnp.stack(k_o), jnp.stack(v_o))

    bp = x_prompt.shape[0]
    dt = x_prompt.dtype
    y_prompt, p_wkv, p_shift, p_pool, p_k, p_v = run(
        x_prompt, c_prompt, 0,
        jnp.zeros((N_A_LAYERS, bp, RWKV_HEADS, RWKV_HEAD, RWKV_HEAD), jnp.float32),
        jnp.zeros((N_A_LAYERS, bp, SHIFT_WIDTH), dt),
        jnp.zeros((N_A_LAYERS, bp, POOL_HIST, POOL_WIDTH), dt),
        jnp.zeros((N_C_LAYERS, bp, 0, SB_HEADS, SB_HEAD), dt),
        jnp.zeros((N_C_LAYERS, bp, 0, SB_HEADS, SB_HEAD), dt))
    y_sample, s_wkv, s_shift, s_pool, s_k, s_v = run(
        x_sample, c_sample, PAST_LEN, state_wkv, state_shift, state_pool, cache_k, cache_v)
    return (y_prompt, y_sample, p_wkv, p_shift, p_pool, p_k, p_v, s_wkv, s_shift, s_pool, s_k, s_v)
```

```python
import functools

import jax
import jax.numpy as jnp
import numpy as np
from jax import lax
from jax.experimental import pallas as pl
from jax.experimental.pallas import tpu as pltpu

F32 = jnp.float32
BF16 = jnp.bfloat16

D = 1024
CHUNK = 64
HEADS = 8
HD = 64
RW = HEADS * HD
PW = D - RW
POOL_WINDOWS = (2, 4, 8, 16)
POOL_HIST = 15
SHIFT_W = 3 * RW + 64 + 64 + 128
N_EXPERTS = 32
TOP_K = 4
EPS = 1e-6
LN_X_EPS = 64e-5
SWIGLU_LIMIT = 7.0
SWIGLU_ALPHA = 1.702

LANES = 128
TM = 256
TME = 256
SCAN_L = 128
VMEM_LIMIT = 56 * 1024 * 1024


def _cparams(sem):
    return pltpu.CompilerParams(dimension_semantics=sem, vmem_limit_bytes=VMEM_LIMIT)


def _softplus(z):
    return jnp.maximum(z, 0.0) + jnp.log1p(jnp.exp(-jnp.abs(z)))


def _split3(x):
    hi = x.astype(BF16)
    r1 = x - hi.astype(F32)
    mid = r1.astype(BF16)
    lo = (r1 - mid.astype(F32)).astype(BF16)
    return hi, mid, lo


def _dot(a, b):
    return jnp.dot(a, b, preferred_element_type=F32)


def _dot3(x, w01):
    hi, mid, lo = _split3(x)
    return _dot(hi, w01) + _dot(mid, w01) + _dot(lo, w01)


def _mod_tile(x_ref, g_ref, sh_ref, sc_ref, h_ref):
    g = g_ref[...]
    for q in range(x_ref.shape[0] // CHUNK):
        x = x_ref[q * CHUNK:(q + 1) * CHUNK, :]
        ms = jnp.mean(x * x, axis=-1, keepdims=True)
        y = x * lax.rsqrt(ms + EPS) * g
        h = y * (1.0 + sc_ref[0, q:q + 1, :]) + sh_ref[0, q:q + 1, :]
        h_ref[q * CHUNK:(q + 1) * CHUNK, :] = h.astype(h_ref.dtype)


def _ada_kernel(c_ref, w_ref, b_ref, o_ref):
    c = c_ref[...]
    s = c * jax.nn.sigmoid(c)
    o_ref[0] = _dot(s.astype(BF16), w_ref[0].astype(BF16)) + b_ref[0]


def ada_mod(c_all, ada_w, ada_b):
    depth, _, n = ada_w.shape
    s = c_all.shape[0]
    tn = 1536
    return pl.pallas_call(
        _ada_kernel,
        grid=(depth, n // tn),
        in_specs=[pl.BlockSpec((s, D), lambda l, j: (0, 0)),
                  pl.BlockSpec((1, D, tn), lambda l, j: (l, 0, j)),
                  pl.BlockSpec((1, 1, tn), lambda l, j: (l, 0, j))],
        out_specs=pl.BlockSpec((1, s, tn), lambda l, j: (l, 0, j)),
        out_shape=jax.ShapeDtypeStruct((depth, s, n), F32),
        compiler_params=_cparams(("arbitrary", "arbitrary")),
        name="ada_mod",
    )(c_all, ada_w, ada_b.reshape(depth, 1, n))


def _normmm_kernel(x_ref, g_ref, sh_ref, sc_ref, w_ref, *rest, widths):
    outs, h_ref = rest[:-1], rest[-1]
    _mod_tile(x_ref, g_ref, sh_ref, sc_ref, h_ref)
    h = h_ref[...]
    off = 0
    for o_ref, wd in zip(outs, widths):
        o_ref[...] = _dot(h, w_ref[:, off:off + wd])
        off += wd


def normmm(x, g, sh, sc, w_bf, widths):
    nt = x.shape[0]
    n = w_bf.shape[1]
    q = TM // CHUNK
    return pl.pallas_call(
        functools.partial(_normmm_kernel, widths=widths),
        grid=(nt // TM,),
        in_specs=[pl.BlockSpec((TM, D), lambda i: (i, 0)),
                  pl.BlockSpec((1, D), lambda i: (0, 0)),
                  pl.BlockSpec((1, q, D), lambda i: (i, 0, 0)),
                  pl.BlockSpec((1, q, D), lambda i: (i, 0, 0)),
                  pl.BlockSpec((D, n), lambda i: (0, 0))],
        out_specs=[pl.BlockSpec((TM, wd), lambda i: (i, 0)) for wd in widths],
        out_shape=[jax.ShapeDtypeStruct((nt, wd), F32) for wd in widths],
        scratch_shapes=[pltpu.VMEM((TM, D), BF16)],
        compiler_params=_cparams(("arbitrary",)),
        name="normmm",
    )(x, g, sh, sc, w_bf)


def _t_pad(x, rows):
    m = x.shape[0]
    if m < rows:
        x = jnp.concatenate([x, jnp.zeros((rows - m, x.shape[1]), x.dtype)], axis=0)
    return x.T


def _prep_kernel(ps_ref, s0_ref, mu_ref, w0_ref, wlora_ref, a0_ref, wgate_ref, kk_ref, ka_ref, rk_ref,
                 bd_ref, r_o, w_o, k_o, a_o, b_o, vt_o, g_o, bonus_o, prev_sc, *, tt, ttp):
    t = pl.program_id(1)

    @pl.when(t == 0)
    def _():
        prev_sc[...] = s0_ref[0]

    ps = ps_ref[0]
    rolled = pltpu.roll(ps, 1, axis=0)
    row = lax.broadcasted_iota(jnp.int32, ps.shape, 0)
    prev = jnp.where(row == 0, prev_sc[...], rolled)
    prev_sc[...] = ps[tt - 1:tt, :]
    xs = ps + (prev - ps) * mu_ref[...]
    r = xs[:, 0:RW]
    k = xs[:, RW:2 * RW]
    v = xs[:, 2 * RW:3 * RW]
    xwa = xs[:, 3 * RW:3 * RW + 128]
    xg = xs[:, 3 * RW + 128:3 * RW + 256]
    lane = lax.broadcasted_iota(jnp.int32, xwa.shape, 1)
    lin = jnp.where(lane < 64, jnp.tanh(xwa), xwa)
    lora = _dot(lin.astype(BF16), wlora_ref[...])
    log_w = -_softplus(-(w0_ref[...] + lora[:, :RW])) - 0.5
    decay = jnp.exp(-jnp.exp(log_w))
    a = jax.nn.sigmoid(a0_ref[...] + lora[:, RW:])
    g_o[0] = _dot(jax.nn.sigmoid(xg).astype(BF16), wgate_ref[...])
    bd = bd_ref[...]
    kk = k * kk_ref[...]
    n2 = _dot3(kk * kk, bd)
    kk = kk * lax.rsqrt(jnp.maximum(n2, 1e-24))
    k_mod = k * (1.0 + (a - 1.0) * ka_ref[...])
    bonus_o[0] = _dot3(r * k_mod * rk_ref[...], bd) * v
    a_vec = -kk
    b_vec = kk * a
    for h in range(HEADS):
        sl = slice(h * HD, (h + 1) * HD)
        r_o[0, h] = r[:, sl]
        w_o[0, h] = decay[:, sl]
        k_o[0, h] = k_mod[:, sl]
        a_o[0, h] = a_vec[:, sl]
        b_o[0, h] = b_vec[:, sl]
    vt_o[0] = _t_pad(v, ttp)


def rwkv_prep(ps, shift0, p, tt):
    s, t_len, _ = ps.shape
    ttp = max(tt, LANES)
    tpad = (t_len // tt) * ttp
    row = lambda a: a.reshape(1, -1)
    const = lambda shape: pl.BlockSpec(shape, lambda i, j: (0,) * len(shape))
    head_spec = pl.BlockSpec((1, HEADS, tt, HD), lambda i, j: (i, 0, j, 0))
    head_shape = jax.ShapeDtypeStruct((s, HEADS, t_len, HD), F32)
    tok_spec = pl.BlockSpec((1, tt, RW), lambda i, j: (i, j, 0))
    tok_shape = jax.ShapeDtypeStruct((s, t_len, RW), F32)
    return pl.pallas_call(
        functools.partial(_prep_kernel, tt=tt, ttp=ttp),
        grid=(s, t_len // tt),
        in_specs=[pl.BlockSpec((1, tt, SHIFT_W), lambda i, j: (i, j, 0)),
                  pl.BlockSpec((1, 1, SHIFT_W), lambda i, j: (i, 0, 0)),
                  const((1, SHIFT_W)), const((1, RW)), const((128, 2 * RW)), const((1, RW)),
                  const((128, RW)), const((1, RW)), const((1, RW)), const((1, RW)), const((RW, RW))],
        out_specs=[head_spec] * 5 + [pl.BlockSpec((1, RW, ttp), lambda i, j: (i, 0, j)), tok_spec, tok_spec],
        out_shape=[head_shape] * 5 + [jax.ShapeDtypeStruct((s, RW, tpad), F32), tok_shape, tok_shape],
        scratch_shapes=[pltpu.VMEM((1, SHIFT_W), F32)],
        compiler_params=_cparams(("arbitrary", "arbitrary")),
        name="rwkv_prep",
    )(ps, shift0.reshape(s, 1, SHIFT_W), row(p["mu"]), row(p["w0"]), p["w_lora"], row(p["a0"]),
      p["w_gate"], row(p["k_k"]), row(p["k_a"]), row(p["r_k"]), p["bd"])


def _scan_kernel(s0_ref, r_ref, w_ref, k_ref, a_ref, b_ref, vt_ref, s_ref, yt_ref, *, steps, group):
    c = pl.program_id(1)

    @pl.when(c == 0)
    def _():
        s_ref[...] = s0_ref[...]

    yt_ref[...] = jnp.zeros_like(yt_ref)
    lane = lax.broadcasted_iota(jnp.int32, (HD, yt_ref.shape[2]), 1)

    def step(t, carry):
        onehot = lane == t
        for g in range(group):
            row = lambda ref: ref[g, pl.ds(t, 1), :]
            st = s_ref[g]
            vcol = jnp.sum(jnp.where(onehot, vt_ref[g], 0.0), axis=1, keepdims=True)
            sa = jnp.sum(st * row(a_ref), axis=1, keepdims=True)
            st = st * row(w_ref) + sa * row(b_ref) + vcol * row(k_ref)
            s_ref[g] = st
            y = jnp.sum(st * row(r_ref), axis=1, keepdims=True)
            yt_ref[g] = jnp.where(onehot, y, yt_ref[g])
        return carry

    lax.fori_loop(0, steps, step, 0)


def wkv_scan(s0, r, w, k, a, b, vt, t_len, group):
    n = s0.shape[0]
    steps = min(t_len, SCAN_L)
    tpad = vt.shape[2]
    vec = pl.BlockSpec((group, steps, HD), lambda i, c: (i, c, 0))
    st = pl.BlockSpec((group, HD, HD), lambda i, c: (i, 0, 0))
    tr = pl.BlockSpec((group, HD, SCAN_L), lambda i, c: (i, 0, c))
    return pl.pallas_call(
        functools.partial(_scan_kernel, steps=steps, group=group),
        grid=(n // group, t_len // steps),
        in_specs=[st, vec, vec, vec, vec, vec, tr],
        out_specs=[st, tr],
        out_shape=[jax.ShapeDtypeStruct((n, HD, HD), F32), jax.ShapeDtypeStruct((n, HD, tpad), F32)],
        compiler_params=_cparams(("arbitrary", "arbitrary")),
        name="wkv_scan",
    )(s0, r, w, k, a, b, vt)


def _post_kernel(yt_ref, g_ref, bonus_ref, pp_ref, pool0_ref, x_ref, gate_ref, lnw_ref, lnb_ref, wpool_ref,
                 pscale_ref, wout_ref, o_ref, ext_sc, *, tt, ttp, pos0):
    t = pl.program_id(1)

    @pl.when(t == 0)
    def _():
        ext_sc[0:16, :] = pool0_ref[0]

    @pl.when(t > 0)
    def _():
        ext_sc[0:16, :] = ext_sc[tt:tt + 16, :]

    pp = pp_ref[0]
    ext_sc[16:16 + tt, :] = pp

    parts = []
    for h in range(HEADS):
        yh = yt_ref[0, h * HD:(h + 1) * HD, :]
        mean = jnp.mean(yh, axis=0, keepdims=True)
        cen = yh - mean
        var = jnp.mean(cen * cen, axis=0, keepdims=True)
        parts.append(cen * lax.rsqrt(var + LN_X_EPS))
    yn = jnp.concatenate(parts, axis=0).T[:tt, :]
    y_rwkv = (yn * lnw_ref[...] + lnb_ref[...] + bonus_ref[0]) * g_ref[0]

    pos = pos0 + t * tt + lax.broadcasted_iota(jnp.int32, (tt, LANES), 0)
    out = _dot(y_rwkv.astype(BF16), wout_ref[0:RW, :])
    gd = PW // len(POOL_WINDOWS)
    for gi, win in enumerate(POOL_WINDOWS):
        lanes = slice(gi * gd, (gi + 1) * gd)
        acc = ext_sc[16:16 + tt, lanes]
        for j in range(1, win):
            acc = acc + ext_sc[16 - j:16 - j + tt, lanes]
        cnt = jnp.minimum(pos + 1, win).astype(F32)
        dlt = acc / cnt - pp[:, lanes]
        yp = _dot(dlt.astype(BF16), wpool_ref[gi]) * pscale_ref[:, lanes]
        out = out + _dot(yp.astype(BF16), wout_ref[RW + gi * gd:RW + (gi + 1) * gd, :])
    o_ref[0] = x_ref[0] + gate_ref[0] * out


def rwkv_post(yt, g, bonus, pp, pool0, x, gate, p, tt, pos0):
    s, t_len, _ = pp.shape
    ttp = max(tt, LANES)
    const = lambda shape: pl.BlockSpec(shape, lambda i, j: (0,) * len(shape))
    tok = lambda wd: pl.BlockSpec((1, tt, wd), lambda i, j: (i, j, 0))
    gd = PW // len(POOL_WINDOWS)
    return pl.pallas_call(
        functools.partial(_post_kernel, tt=tt, ttp=ttp, pos0=pos0),
        grid=(s, t_len // tt),
        in_specs=[pl.BlockSpec((1, RW, ttp), lambda i, j: (i, 0, j)), tok(RW), tok(RW), tok(PW),
                  pl.BlockSpec((1, 16, PW), lambda i, j: (i, 0, 0)), tok(D),
                  pl.BlockSpec((1, 1, D), lambda i, j: (i, 0, 0)),
                  const((1, RW)), const((1, RW)), const((len(POOL_WINDOWS), gd, gd)), const((1, PW)),
                  const((D, D))],
        out_specs=tok(D),
        out_shape=jax.ShapeDtypeStruct((s, t_len, D), F32),
        scratch_shapes=[pltpu.VMEM((16 + tt, PW), F32)],
        compiler_params=_cparams(("arbitrary", "arbitrary")),
        name="rwkv_post",
    )(yt, g, bonus, pp, pool0, x, gate, p["ln_w"].reshape(1, RW), p["ln_b"].reshape(1, RW), p["w_pool"],
      p["pool_scale"].reshape(1, PW), p["w_out"])


def _sb_block(q_h, k_bf, v_bf, u_ref, carry_ref, acc_ref, h, valid):
    tk = k_bf.shape[0]
    logit = lax.dot_general(q_h, k_bf, (((1,), (1,)), ((), ())), preferred_element_type=F32)
    sp = _softplus(logit)
    log_1m = -sp
    if valid is not None:
        log_1m = jnp.where(valid, log_1m, 0.0)
    hi = log_1m.astype(BF16)
    lo = (log_1m - hi.astype(F32)).astype(BF16)
    u = u_ref[...]
    cs = _dot(hi, u) + _dot(lo, u)
    wts = jnp.exp(logit - sp + cs[:, :tk] + carry_ref[h][:, :tk])
    if valid is not None:
        wts = jnp.where(valid, wts, 0.0)
    acc_ref[h] += _dot(wts.astype(BF16), v_bf)
    carry_ref[h] += cs[:, tk:]


def _sb_kernel(*refs, tq, n_cache, tkc):
    if n_cache:
        q_ref, kn_ref, vn_ref, un_ref, kc_ref, vc_ref, uc_ref, o_ref, carry_ref, acc_ref = refs
    else:
        q_ref, kn_ref, vn_ref, un_ref, o_ref, carry_ref, acc_ref = refs
    qi = pl.program_id(2)
    lane = lax.broadcasted_iota(jnp.int32, (tq, LANES), 1)
    q2 = q_ref[0] * (HD ** -0.5)
    q_heads = (jnp.where(lane < HD, q2, 0.0).astype(BF16), jnp.where(lane >= HD, q2, 0.0).astype(BF16))
    carry_ref[...] = jnp.zeros_like(carry_ref)
    acc_ref[...] = jnp.zeros_like(acc_ref)

    valid = (lax.broadcasted_iota(jnp.int32, (tq, tq), 1) < lax.broadcasted_iota(jnp.int32, (tq, tq), 0))
    start = pl.multiple_of(qi * tq, tq)
    k_bf = kn_ref[0, pl.ds(start, tq), :].astype(BF16)
    v_bf = vn_ref[0, pl.ds(start, tq), :].astype(BF16)
    for h in range(2):
        _sb_block(q_heads[h], k_bf, v_bf, un_ref, carry_ref, acc_ref, h, valid)

    def earlier(it, c):
        s0 = pl.multiple_of((qi - 1 - it) * tq, tq)
        kb = kn_ref[0, pl.ds(s0, tq), :].astype(BF16)
        vb = vn_ref[0, pl.ds(s0, tq), :].astype(BF16)
        for h in range(2):
            _sb_block(q_heads[h], kb, vb, un_ref, carry_ref, acc_ref, h, None)
        return c

    lax.fori_loop(0, qi, earlier, 0)

    if n_cache:
        def cached(it, c):
            s0 = pl.multiple_of((n_cache - 1 - it) * tkc, tkc)
            kb = kc_ref[0, pl.ds(s0, tkc), :].astype(BF16)
            vb = vc_ref[0, pl.ds(s0, tkc), :].astype(BF16)
            for h in range(2):
                _sb_block(q_heads[h], kb, vb, uc_ref, carry_ref, acc_ref, h, None)
            return c

        lax.fori_loop(0, n_cache, cached, 0)

    o_ref[0] = jnp.where(lane < HD, acc_ref[0], acc_ref[1])


def _cumsum_matrix(tk):
    j = np.arange(tk)[:, None]
    s = np.arange(tk)[None, :]
    return jnp.asarray(np.concatenate([(j > s), np.ones((tk, LANES), bool)], axis=1), BF16)


def stick_breaking(q, k, v, ck, cv, tq):
    b, t_len, _ = q.shape
    tkc = 128
    n_cache = 0 if ck is None else ck.shape[1] // tkc
    pair = lambda rows: pl.BlockSpec((1, rows, LANES), lambda i, hp, qi: (i, 0, hp))
    const = lambda shape: pl.BlockSpec(shape, lambda i, hp, qi: (0, 0))
    qspec = pl.BlockSpec((1, tq, LANES), lambda i, hp, qi: (i, qi, hp))
    in_specs = [qspec, pair(t_len), pair(t_len), const((tq, tq + LANES))]
    args = [q, k, v, _cumsum_matrix(tq)]
    if n_cache:
        in_specs += [pair(ck.shape[1]), pair(ck.shape[1]), const((tkc, tkc + LANES))]
        args += [ck, cv, _cumsum_matrix(tkc)]
    return pl.pallas_call(
        functools.partial(_sb_kernel, tq=tq, n_cache=n_cache, tkc=tkc),
        grid=(b, D // LANES, t_len // tq),
        in_specs=in_specs,
        out_specs=qspec,
        out_shape=jax.ShapeDtypeStruct((b, t_len, D), F32),
        scratch_shapes=[pltpu.VMEM((2, tq, LANES), F32), pltpu.VMEM((2, tq, LANES), F32)],
        compiler_params=_cparams(("arbitrary", "arbitrary", "arbitrary")),
        name="stick_breaking",
    )(*args)


def _proj_res_kernel(a_ref, w_ref, x_ref, gate_ref, o_ref):
    y = _dot(a_ref[...].astype(BF16), w_ref[...])
    for q in range(TM // CHUNK):
        rows = slice(q * CHUNK, (q + 1) * CHUNK)
        o_ref[rows, :] = x_ref[rows, :] + gate_ref[0, q:q + 1, :] * y[rows, :]


def proj_residual(a, w_bf, x, gate):
    nt = x.shape[0]
    q = TM // CHUNK
    return pl.pallas_call(
        _proj_res_kernel,
        grid=(nt // TM,),
        in_specs=[pl.BlockSpec((TM, D), lambda i: (i, 0)), pl.BlockSpec((D, D), lambda i: (0, 0)),
                  pl.BlockSpec((TM, D), lambda i: (i, 0)), pl.BlockSpec((1, q, D), lambda i: (i, 0, 0))],
        out_specs=pl.BlockSpec((TM, D), lambda i: (i, 0)),
        out_shape=jax.ShapeDtypeStruct((nt, D), F32),
        compiler_params=_cparams(("arbitrary",)),
        name="proj_residual",
    )(a, w_bf, x, gate)


def _route_kernel(x_ref, g_ref, sh_ref, sc_ref, wr_ref, br_ref, tri_ref, h_o, mi_o, mf_o, cnt_o, carry_sc):
    i = pl.program_id(0)

    @pl.when(i == 0)
    def _():
        carry_sc[...] = jnp.zeros_like(carry_sc)

    _mod_tile(x_ref, g_ref, sh_ref, sc_ref, h_o)
    h = h_o[...]
    h_hi = h.astype(BF16)
    h_lo = (h - h_hi.astype(F32)).astype(BF16)
    w_hi = wr_ref[0]
    w_lo = wr_ref[1]
    lg = _dot(h_hi, w_hi) + _dot(h_hi, w_lo) + _dot(h_lo, w_hi) + br_ref[...]
    lane = lax.broadcasted_iota(jnp.int32, lg.shape, 1)
    lane_f = lane.astype(F32)
    neg = jnp.float32(-3.0e38)
    vals, idxs = [], []
    mask = jnp.zeros(lg.shape, F32)
    for _ in range(TOP_K):
        m = jnp.max(lg, axis=1, keepdims=True)
        idx = jnp.min(jnp.where(lg == m, lane_f, float(LANES)), axis=1, keepdims=True)
        sel = lane_f == idx
        mask = jnp.where(sel, 1.0, mask)
        lg = jnp.where(sel, neg, lg)
        vals.append(m)
        idxs.append(idx)
    es = [jnp.exp(vj - vals[0]) for vj in vals]
    den = es[0] + es[1] + es[2] + es[3]
    rank = _dot(tri_ref[...], mask.astype(BF16)) + carry_sc[...]
    carry_sc[...] += jnp.sum(mask, axis=0, keepdims=True)
    mi = jnp.zeros(lg.shape, F32)
    mf = jnp.zeros(lg.shape, F32)
    for j in range(TOP_K):
        rj = jnp.sum(jnp.where(lane_f == idxs[j], rank, 0.0), axis=1, keepdims=True)
        mi = jnp.where(lane == j, idxs[j], mi)
        mi = jnp.where(lane == TOP_K + j, rj, mi)
        mf = jnp.where(lane == j, es[j] / den, mf)
    mi_o[...] = mi.astype(jnp.int32)
    mf_o[...] = mf
    cnt_o[...] = jnp.broadcast_to(carry_sc[...], cnt_o.shape)


def moe_route(x, g, sh, sc, wr2, br, tri):
    nt = x.shape[0]
    q = TM // CHUNK
    return pl.pallas_call(
        _route_kernel,
        grid=(nt // TM,),
        in_specs=[pl.BlockSpec((TM, D), lambda i: (i, 0)), pl.BlockSpec((1, D), lambda i: (0, 0)),
                  pl.BlockSpec((1, q, D), lambda i: (i, 0, 0)), pl.BlockSpec((1, q, D), lambda i: (i, 0, 0)),
                  pl.BlockSpec((2, D, LANES), lambda i: (0, 0, 0)), pl.BlockSpec((1, LANES), lambda i: (0, 0)),
                  pl.BlockSpec((TM, TM), lambda i: (0, 0))],
        out_specs=[pl.BlockSpec((TM, D), lambda i: (i, 0)), pl.BlockSpec((TM, LANES), lambda i: (i, 0)),
                   pl.BlockSpec((TM, LANES), lambda i: (i, 0)), pl.BlockSpec((8, LANES), lambda i: (0, 0))],
        out_shape=[jax.ShapeDtypeStruct((nt, D), F32), jax.ShapeDtypeStruct((nt, LANES), jnp.int32),
                   jax.ShapeDtypeStruct((nt, LANES), F32), jax.ShapeDtypeStruct((8, LANES), F32)],
        scratch_shapes=[pltpu.VMEM((1, LANES), F32)],
        compiler_params=_cparams(("arbitrary",)),
        name="moe_route",
    )(x, g, sh, sc, wr2, br, tri)


def _dispatch_kernel(dest_ref, h_ref, xs_in, xs_out, sem):
    del xs_in

    def copy(r, j):
        return pltpu.make_async_copy(h_ref.at[pl.ds(r, 1)], xs_out.at[pl.ds(dest_ref[r * TOP_K + j], 1)], sem)

    def issue(r, c):
        for j in range(TOP_K):
            copy(r, j).start()
        return c

    def drain(r, c):
        for j in range(TOP_K):
            copy(r, j).wait()
        return c

    lax.fori_loop(0, TM, issue, 0)
    lax.fori_loop(0, TM, drain, 0)


def moe_dispatch(dest_flat, h, xs_zero):
    nt = h.shape[0]
    return pl.pallas_call(
        _dispatch_kernel,
        grid=(nt // TM,),
        in_specs=[pl.BlockSpec((TM * TOP_K,), lambda i: (i,), memory_space=pltpu.SMEM),
                  pl.BlockSpec((TM, D), lambda i: (i, 0)),
                  pl.BlockSpec(memory_space=pl.ANY)],
        out_specs=pl.BlockSpec(memory_space=pl.ANY),
        out_shape=jax.ShapeDtypeStruct(xs_zero.shape, xs_zero.dtype),
        scratch_shapes=[pltpu.SemaphoreType.DMA(())],
        input_output_aliases={2: 0},
        compiler_params=_cparams(("arbitrary",)),
        name="moe_dispatch",
    )(dest_flat, h, xs_zero)


def _expert_kernel(te_ref, nu_ref, x_ref, wg_ref, wl_ref, bg_ref, bl_ref, wd_ref, bd_ref, o_ref):
    i = pl.program_id(0)

    @pl.when(i < nu_ref[0])
    def _():
        x = x_ref[...].astype(BF16)
        glu = jnp.minimum(_dot(x, wg_ref[0]) + bg_ref[0], SWIGLU_LIMIT)
        lin = jnp.clip(_dot(x, wl_ref[0]) + bl_ref[0], -SWIGLU_LIMIT, SWIGLU_LIMIT)
        act = glu * jax.nn.sigmoid(SWIGLU_ALPHA * glu) * (lin + 1.0)
        o_ref[...] = _dot(act.astype(BF16), wd_ref[0]) + bd_ref[0]

    @pl.when(i >= nu_ref[0])
    def _():
        o_ref[...] = jnp.zeros_like(o_ref)


def moe_experts(tile_expert, n_used, xs, wg, wl, bg, bl, wd, bd):
    rows = xs.shape[0]
    wspec = pl.BlockSpec((1, D, D), lambda i, te, nu: (te[i], 0, 0))
    bspec = pl.BlockSpec((1, 1, D), lambda i, te, nu: (te[i], 0, 0))
    xspec = pl.BlockSpec((TME, D), lambda i, te, nu: (i, 0))
    return pl.pallas_call(
        _expert_kernel,
        grid_spec=pltpu.PrefetchScalarGridSpec(
            num_scalar_prefetch=2, grid=(rows // TME,),
            in_specs=[xspec, wspec, wspec, bspec, bspec, wspec, bspec],
            out_specs=xspec),
        out_shape=jax.ShapeDtypeStruct((rows, D), F32),
        compiler_params=_cparams(("arbitrary",)),
        name="moe_experts",
    )(tile_expert, n_used, xs, wg, wl, bg, bl, wd, bd)


def _combine_kernel(dest_ref, x_ref, mf_ref, gate_ref, ys_ref, o_ref, buf, sem):
    def copy(r, j):
        return pltpu.make_async_copy(ys_ref.at[pl.ds(dest_ref[r * TOP_K + j], 1)], buf.at[j, pl.ds(r, 1)], sem)

    def issue(r, c):
        for j in range(TOP_K):
            copy(r, j).start()
        return c

    def drain(r, c):
        for j in range(TOP_K):
            copy(r, j).wait()
        return c

    lax.fori_loop(0, TM, issue, 0)
    lax.fori_loop(0, TM, drain, 0)
    mf = mf_ref[...]
    y = buf[0] * mf[:, 0:1]
    for j in range(1, TOP_K):
        y = y + buf[j] * mf[:, j:j + 1]
    for q in range(TM // CHUNK):
        rows = slice(q * CHUNK, (q + 1) * CHUNK)
        o_ref[rows, :] = x_ref[rows, :] + gate_ref[0, q:q + 1, :] * y[rows, :]


def moe_combine(dest_flat, x, mf, gate, ys):
    nt = x.shape[0]
    q = TM // CHUNK
    return pl.pallas_call(
        _combine_kernel,
        grid=(nt // TM,),
        in_specs=[pl.BlockSpec((TM * TOP_K,), lambda i: (i,), memory_space=pltpu.SMEM),
                  pl.BlockSpec((TM, D), lambda i: (i, 0)), pl.BlockSpec((TM, LANES), lambda i: (i, 0)),
                  pl.BlockSpec((1, q, D), lambda i: (i, 0, 0)), pl.BlockSpec(memory_space=pl.ANY)],
        out_specs=pl.BlockSpec((TM, D), lambda i: (i, 0)),
        out_shape=jax.ShapeDtypeStruct((nt, D), F32),
        scratch_shapes=[pltpu.VMEM((TOP_K, TM, D), F32), pltpu.SemaphoreType.DMA(())],
        compiler_params=_cparams(("arbitrary",)),
        name="moe_combine",
    )(dest_flat, x, mf, gate, ys)


def moe_layer(x, g, sh, sc, gate, mp):
    nt = x.shape[0]
    h, mi, mf, cnt = moe_route(x, g, sh, sc, mp["wr2"], mp["br"], mp["tri"])
    counts = cnt[0, :N_EXPERTS].astype(jnp.int32)
    padded = (counts + TME - 1) // TME * TME
    pad_end = jnp.cumsum(padded)
    pad_start = pad_end - padded
    dest = pad_start[mi[:, :TOP_K]] + mi[:, TOP_K:2 * TOP_K]
    dest_flat = dest.reshape(-1).astype(jnp.int32)
    n_tiles = (nt * TOP_K + N_EXPERTS * (TME - 1)) // TME
    tile_expert = jnp.minimum(
        jnp.searchsorted(pad_end, jnp.arange(n_tiles, dtype=jnp.int32) * TME, side="right"),
        N_EXPERTS - 1).astype(jnp.int32)
    n_used = (pad_end[-1:] // TME).astype(jnp.int32)
    xs = moe_dispatch(dest_flat, h, jnp.zeros((n_tiles * TME, D), F32))
    ys = moe_experts(tile_expert, n_used, xs, mp["wg"], mp["wl"], mp["bg"], mp["bl"], mp["wd"], mp["bd"])
    return moe_combine(dest_flat, x, mf, gate, ys)


def _final_kernel(x_ref, g_ref, o_ref):
    x = x_ref[...]
    ms = jnp.mean(x * x, axis=-1, keepdims=True)
    o_ref[...] = x * lax.rsqrt(ms + EPS) * g_ref[...]


def final_norm(x, g):
    nt = x.shape[0]
    return pl.pallas_call(
        _final_kernel,
        grid=(nt // TM,),
        in_specs=[pl.BlockSpec((TM, D), lambda i: (i, 0)), pl.BlockSpec((1, D), lambda i: (0, 0))],
        out_specs=pl.BlockSpec((TM, D), lambda i: (i, 0)),
        out_shape=jax.ShapeDtypeStruct((nt, D), F32),
        compiler_params=_cparams(("arbitrary",)),
        name="final_norm",
    )(x, g.reshape(1, D))


def rwkv_pool_layer(x, mods, g, lp, groups, state):
    sh, sc, gate_rows = mods
    ps, pp = normmm(x, g, sh, sc, lp["w_in"], (SHIFT_W, PW))
    outs, states = [], []
    for (r0, s, t_len, tt, pos0, group), (wkv0, shift0, pool0) in zip(groups, state):
        n = s * t_len
        ps_g = ps[r0:r0 + n].reshape(s, t_len, SHIFT_W)
        pp_g = pp[r0:r0 + n].reshape(s, t_len, PW)
        r, w, k, a, b, vt, gt, bonus = rwkv_prep(ps_g, shift0, lp, tt)
        flat = lambda z: z.reshape(s * HEADS, t_len, HD)
        s_new, yt = wkv_scan(wkv0.reshape(s * HEADS, HD, HD), flat(r), flat(w), flat(k), flat(a), flat(b),
                             vt.reshape(s * HEADS, HD, vt.shape[2]), t_len, group)
        pool_pad = jnp.concatenate([jnp.zeros((s, 1, PW), F32), pool0], axis=1)
        x_g = x[r0:r0 + n].reshape(s, t_len, D)
        out = rwkv_post(yt.reshape(s, RW, yt.shape[2]), gt, bonus, pp_g, pool_pad, x_g,
                        gate_rows[r0 // CHUNK:(r0 + n) // CHUNK:t_len // CHUNK].reshape(s, 1, D), lp, tt, pos0)
        outs.append(out.reshape(n, D))
        states.append((s_new.reshape(s, HEADS, HD, HD), ps_g[:, -1], pp_g[:, t_len - POOL_HIST:]))
    return jnp.concatenate(outs, axis=0), states


def attention_layer(x, mods, g, lp, groups, caches):
    sh, sc, gate = mods
    q, k, v = normmm(x, g, sh, sc, lp["w_qkv"], (D, D, D))
    outs, kvs = [], []
    for (r0, s, t_len, tq), cache in zip(groups, caches):
        n = s * t_len
        seq = lambda z: z[r0:r0 + n].reshape(s, t_len, D)
        ck, cv = cache
        outs.append(stick_breaking(seq(q), seq(k), seq(v), ck, cv, tq).reshape(n, D))
        kvs.append((seq(k).reshape(s, t_len, D // HD, HD), seq(v).reshape(s, t_len, D // HD, HD)))
    o = jnp.concatenate(outs, axis=0)
    return proj_residual(o, lp["w_out"], x, gate), kvs


def kernel(x_prompt, x_sample, c_prompt, c_sample, state_wkv, state_shift, state_pool, cache_k, cache_v, ada_w, ada_b, norm_mix_g, norm_ffn_g, final_g, a_w_in, a_mu, a_w0, a_w_decay, a_a0, a_w_iclr, a_w_gate, a_k_k, a_k_a, a_r_k, a_ln_w, a_ln_b, b_w_pool, b_pool_scale, ab_w_out, c_w_qkv, c_w_out, moe_w_router, moe_b_router, moe_w_up, moe_b_up, moe_w_down, moe_b_down):
    bp, tp, _ = x_prompt.shape
    bs, ts, _ = x_sample.shape
    past = cache_k.shape[2]
    n_p, n_s = bp * tp, bs * ts
    nt = n_p + n_s
    x = jnp.concatenate([x_prompt.reshape(n_p, D), x_sample.reshape(n_s, D)], axis=0)

    mod = ada_mod(jnp.concatenate([c_prompt, c_sample], axis=0), ada_w, ada_b)
    seq_of_chunk = np.concatenate([np.repeat(np.arange(bp), tp // CHUNK), bp + np.repeat(np.arange(bs), ts // CHUNK)])
    q = TM // CHUNK

    def mod_rows(layer, idx):
        rows = mod[layer, :, idx * D:(idx + 1) * D][seq_of_chunk]
        return rows, rows.reshape(nt // TM, q, D)

    zeros = lambda *shape: jnp.zeros(shape, F32)
    bd = jnp.asarray(np.kron(np.eye(HEADS), np.ones((HD, HD))), BF16)
    tri = jnp.asarray(np.tril(np.ones((TM, TM)), -1), BF16)

    def moe_params(layer):
        wr = jnp.pad(moe_w_router[layer], ((0, 0), (0, LANES - N_EXPERTS)))
        wr_hi = wr.astype(BF16)
        wr_lo = (wr - wr_hi.astype(F32)).astype(BF16)
        br = jnp.concatenate([moe_b_router[layer], jnp.full((LANES - N_EXPERTS,), -1e30, F32)]).reshape(1, LANES)
        w_up = moe_w_up[layer].astype(BF16)
        b_up = moe_b_up[layer]
        return dict(wr2=jnp.stack([wr_hi, wr_lo]), br=br, tri=tri,
                    wg=w_up[:, :, 0::2], wl=w_up[:, :, 1::2],
                    bg=b_up[:, None, 0::2], bl=b_up[:, None, 1::2],
                    wd=moe_w_down[layer].astype(BF16), bd=moe_b_down[layer][:, None, :])

    w_lora = jnp.zeros((128, 2 * RW), F32).at[:64, :RW].set(a_w_decay[0]).at[64:, RW:].set(a_w_iclr[0])
    lp0 = dict(w_in=a_w_in[0].astype(BF16), mu=a_mu[0], w0=a_w0[0], w_lora=w_lora.astype(BF16), a0=a_a0[0],
               w_gate=a_w_gate[0].astype(BF16), k_k=a_k_k[0], k_a=a_k_a[0], r_k=a_r_k[0], bd=bd,
               ln_w=a_ln_w[0], ln_b=a_ln_b[0], w_pool=b_w_pool[0].astype(BF16), pool_scale=b_pool_scale[0],
               w_out=ab_w_out[0].astype(BF16))
    lp1 = dict(w_qkv=c_w_qkv[0].astype(BF16), w_out=c_w_out[0].astype(BF16))

    sh1, sc1, g1 = (mod_rows(0, i) for i in (0, 1, 2))
    groups0 = [(0, bp, tp, 256, 0, 16), (n_p, bs, ts, ts, past, 16)]
    state0 = [(zeros(bp, HEADS, HD, HD), zeros(bp, SHIFT_W), zeros(bp, POOL_HIST, PW)),
              (state_wkv[0], state_shift[0], state_pool[0])]
    x, st0 = rwkv_pool_layer(x, (sh1[1], sc1[1], g1[0]), norm_mix_g[0].reshape(1, D), lp0, groups0, state0)
    sh2, sc2, g2 = (mod_rows(0, i) for i in (3, 4, 5))
    x = moe_layer(x, norm_ffn_g[0].reshape(1, D), sh2[1], sc2[1], g2[1], moe_params(0))

    sh1, sc1, g1 = (mod_rows(1, i) for i in (0, 1, 2))
    groups1 = [(0, bp, tp, 128), (n_p, bs, ts, ts)]
    caches = [(None, None), (cache_k[0].reshape(bs, past, D), cache_v[0].reshape(bs, past, D))]
    x, kv1 = attention_layer(x, (sh1[1], sc1[1], g1[1]), norm_mix_g[1].reshape(1, D), lp1, groups1, caches)
    sh2, sc2, g2 = (mod_rows(1, i) for i in (3, 4, 5))
    x = moe_layer(x, norm_ffn_g[1].reshape(1, D), sh2[1], sc2[1], g2[1], moe_params(1))

    y = final_norm(x, final_g)
    lead = lambda z: z[None]
    (p_wkv, p_shift, p_pool), (s_wkv, s_shift, s_pool) = st0
    (p_k, p_v), (s_k, s_v) = kv1
    return (y[:n_p].reshape(bp, tp, D), y[n_p:].reshape(bs, ts, D),
            lead(p_wkv), lead(p_shift), lead(p_pool), lead(p_k), lead(p_v),
            lead(s_wkv), lead(s_shift), lead(s_pool), lead(s_k), lead(s_v))
```

```python
import functools

import jax
import jax.numpy as jnp
import numpy as np
from jax import lax
from jax.experimental import pallas as pl
from jax.experimental.pallas import tpu as pltpu

F32 = jnp.float32
BF16 = jnp.bfloat16

D = 1024
CHUNK = 64
HEADS = 8
HD = 64
RW = HEADS * HD
PW = D - RW
POOL_WINDOWS = (2, 4, 8, 16)
POOL_HIST = 15
SHIFT_W = 3 * RW + 64 + 64 + 128
N_EXPERTS = 32
TOP_K = 4
EPS = 1e-6
LN_X_EPS = 64e-5
SWIGLU_LIMIT = 7.0
SWIGLU_ALPHA = 1.702

LANES = 128
TM = 256
TME = 256
SCAN_L = 128
VMEM_LIMIT = 56 * 1024 * 1024


def _cparams(sem):
    return pltpu.CompilerParams(dimension_semantics=sem, vmem_limit_bytes=VMEM_LIMIT)


def _softplus(z):
    return jnp.maximum(z, 0.0) + jnp.log(1.0 + jnp.exp(-jnp.abs(z)))


def _split3(x):
    hi = x.astype(BF16)
    r1 = x - hi.astype(F32)
    mid = r1.astype(BF16)
    lo = (r1 - mid.astype(F32)).astype(BF16)
    return hi, mid, lo


def _dot(a, b):
    return jnp.dot(a, b, preferred_element_type=F32)


def _dot3(x, w01):
    hi, mid, lo = _split3(x)
    return _dot(hi, w01) + _dot(mid, w01) + _dot(lo, w01)


def _mod_tile(x_ref, g_ref, sh_ref, sc_ref, h_ref):
    g = g_ref[...]
    for q in range(x_ref.shape[0] // CHUNK):
        x = x_ref[q * CHUNK:(q + 1) * CHUNK, :]
        ms = jnp.mean(x * x, axis=-1, keepdims=True)
        y = x * lax.rsqrt(ms + EPS) * g
        h = y * (1.0 + sc_ref[0, q:q + 1, :]) + sh_ref[0, q:q + 1, :]
        h_ref[q * CHUNK:(q + 1) * CHUNK, :] = h.astype(h_ref.dtype)


def _ada_kernel(c_ref, w_ref, b_ref, o_ref):
    c = c_ref[...]
    s = c * jax.nn.sigmoid(c)
    o_ref[0] = _dot(s.astype(BF16), w_ref[0].astype(BF16)) + b_ref[0]


def ada_mod(c_all, ada_w, ada_b):
    depth, _, n = ada_w.shape
    s = c_all.shape[0]
    tn = 1536
    return pl.pallas_call(
        _ada_kernel,
        grid=(depth, n // tn),
        in_specs=[pl.BlockSpec((s, D), lambda l, j: (0, 0)),
                  pl.BlockSpec((1, D, tn), lambda l, j: (l, 0, j)),
                  pl.BlockSpec((1, 1, tn), lambda l, j: (l, 0, j))],
        out_specs=pl.BlockSpec((1, s, tn), lambda l, j: (l, 0, j)),
        out_shape=jax.ShapeDtypeStruct((depth, s, n), F32),
        compiler_params=_cparams(("arbitrary", "arbitrary")),
        name="ada_mod",
    )(c_all, ada_w, ada_b.reshape(depth, 1, n))


def _normmm_kernel(x_ref, g_ref, sh_ref, sc_ref, w_ref, *rest, widths):
    outs, h_ref = rest[:-1], rest[-1]
    _mod_tile(x_ref, g_ref, sh_ref, sc_ref, h_ref)
    h = h_ref[...]
    off = 0
    for o_ref, wd in zip(outs, widths):
        o_ref[...] = _dot(h, w_ref[:, off:off + wd])
        off += wd


def normmm(x, g, sh, sc, w_bf, widths):
    nt = x.shape[0]
    n = w_bf.shape[1]
    q = TM // CHUNK
    return pl.pallas_call(
        functools.partial(_normmm_kernel, widths=widths),
        grid=(nt // TM,),
        in_specs=[pl.BlockSpec((TM, D), lambda i: (i, 0)),
                  pl.BlockSpec((1, D), lambda i: (0, 0)),
                  pl.BlockSpec((1, q, D), lambda i: (i, 0, 0)),
                  pl.BlockSpec((1, q, D), lambda i: (i, 0, 0)),
                  pl.BlockSpec((D, n), lambda i: (0, 0))],
        out_specs=[pl.BlockSpec((TM, wd), lambda i: (i, 0)) for wd in widths],
        out_shape=[jax.ShapeDtypeStruct((nt, wd), F32) for wd in widths],
        scratch_shapes=[pltpu.VMEM((TM, D), BF16)],
        compiler_params=_cparams(("arbitrary",)),
        name="normmm",
    )(x, g, sh, sc, w_bf)


def _t_pad(x, rows):
    m = x.shape[0]
    if m < rows:
        x = jnp.concatenate([x, jnp.zeros((rows - m, x.shape[1]), x.dtype)], axis=0)
    return x.T


def _prep_kernel(ps_ref, s0_ref, mu_ref, w0_ref, wlora_ref, a0_ref, wgate_ref, kk_ref, ka_ref, rk_ref,
                 bd_ref, r_o, w_o, k_o, a_o, b_o, vt_o, g_o, bonus_o, prev_sc, *, tt, ttp):
    t = pl.program_id(1)

    @pl.when(t == 0)
    def _():
        prev_sc[...] = s0_ref[0]

    ps = ps_ref[0]
    rolled = pltpu.roll(ps, 1, axis=0)
    row = lax.broadcasted_iota(jnp.int32, ps.shape, 0)
    prev = jnp.where(row == 0, prev_sc[...], rolled)
    prev_sc[...] = ps[tt - 1:tt, :]
    xs = ps + (prev - ps) * mu_ref[...]
    r = xs[:, 0:RW]
    k = xs[:, RW:2 * RW]
    v = xs[:, 2 * RW:3 * RW]
    xwa = xs[:, 3 * RW:3 * RW + 128]
    xg = xs[:, 3 * RW + 128:3 * RW + 256]
    lane = lax.broadcasted_iota(jnp.int32, xwa.shape, 1)
    lin = jnp.where(lane < 64, jnp.tanh(xwa), xwa)
    lora = _dot(lin.astype(BF16), wlora_ref[...])
    log_w = -_softplus(-(w0_ref[...] + lora[:, :RW])) - 0.5
    decay = jnp.exp(-jnp.exp(log_w))
    a = jax.nn.sigmoid(a0_ref[...] + lora[:, RW:])
    g_o[0] = _dot(jax.nn.sigmoid(xg).astype(BF16), wgate_ref[...])
    bd = bd_ref[...]
    kk = k * kk_ref[...]
    n2 = _dot3(kk * kk, bd)
    kk = kk * lax.rsqrt(jnp.maximum(n2, 1e-24))
    k_mod = k * (1.0 + (a - 1.0) * ka_ref[...])
    bonus_o[0] = _dot3(r * k_mod * rk_ref[...], bd) * v
    a_vec = -kk
    b_vec = kk * a
    for h in range(HEADS):
        sl = slice(h * HD, (h + 1) * HD)
        r_o[0, h] = r[:, sl]
        w_o[0, h] = decay[:, sl]
        k_o[0, h] = k_mod[:, sl]
        a_o[0, h] = a_vec[:, sl]
        b_o[0, h] = b_vec[:, sl]
    vt_o[0] = _t_pad(v, ttp)


def rwkv_prep(ps, shift0, p, tt):
    s, t_len, _ = ps.shape
    ttp = max(tt, LANES)
    tpad = (t_len // tt) * ttp
    row = lambda a: a.reshape(1, -1)
    const = lambda shape: pl.BlockSpec(shape, lambda i, j: (0,) * len(shape))
    head_spec = pl.BlockSpec((1, HEADS, tt, HD), lambda i, j: (i, 0, j, 0))
    head_shape = jax.ShapeDtypeStruct((s, HEADS, t_len, HD), F32)
    tok_spec = pl.BlockSpec((1, tt, RW), lambda i, j: (i, j, 0))
    tok_shape = jax.ShapeDtypeStruct((s, t_len, RW), F32)
    return pl.pallas_call(
        functools.partial(_prep_kernel, tt=tt, ttp=ttp),
        grid=(s, t_len // tt),
        in_specs=[pl.BlockSpec((1, tt, SHIFT_W), lambda i, j: (i, j, 0)),
                  pl.BlockSpec((1, 1, SHIFT_W), lambda i, j: (i, 0, 0)),
                  const((1, SHIFT_W)), const((1, RW)), const((128, 2 * RW)), const((1, RW)),
                  const((128, RW)), const((1, RW)), const((1, RW)), const((1, RW)), const((RW, RW))],
        out_specs=[head_spec] * 5 + [pl.BlockSpec((1, RW, ttp), lambda i, j: (i, 0, j)), tok_spec, tok_spec],
        out_shape=[head_shape] * 5 + [jax.ShapeDtypeStruct((s, RW, tpad), F32), tok_shape, tok_shape],
        scratch_shapes=[pltpu.VMEM((1, SHIFT_W), F32)],
        compiler_params=_cparams(("arbitrary", "arbitrary")),
        name="rwkv_prep",
    )(ps, shift0.reshape(s, 1, SHIFT_W), row(p["mu"]), row(p["w0"]), p["w_lora"], row(p["a0"]),
      p["w_gate"], row(p["k_k"]), row(p["k_a"]), row(p["r_k"]), p["bd"])


def _scan_kernel(s0_ref, r_ref, w_ref, k_ref, a_ref, b_ref, vt_ref, s_ref, y_ref, vsplit_sc, vbc_sc, *, steps, group):
    c = pl.program_id(1)

    @pl.when(c == 0)
    def _():
        s_ref[...] = s0_ref[...]

    for g in range(group):
        vt = vt_ref[g]
        hi = vt.astype(BF16)
        vsplit_sc[g, :, 0:SCAN_L] = hi
        vsplit_sc[g, :, SCAN_L:2 * SCAN_L] = (vt - hi.astype(F32)).astype(BF16)
    y_ref[...] = jnp.zeros_like(y_ref)
    sel_row = jnp.bitwise_and(lax.broadcasted_iota(jnp.int32, (2 * SCAN_L, 2 * LANES), 0), SCAN_L - 1)
    sel_blk = lax.broadcasted_iota(jnp.int32, (2 * SCAN_L, 2 * LANES), 1) // LANES
    row16 = lax.broadcasted_iota(jnp.int32, (16, HD), 0)
    last = (((1,), (1,)), ((), ()))

    def one_step(t, u):
        sub = jnp.bitwise_and(t, 7)
        base = pl.multiple_of(t - sub, 8)
        for g in range(group):
            row = lambda ref: ref[g, pl.ds(t, 1), :]
            st = s_ref[g]
            sa = jnp.sum(st * row(a_ref), axis=1, keepdims=True)
            st = st * row(w_ref) + sa * row(b_ref) + vbc_sc[g][:, u * LANES:u * LANES + HD] * row(k_ref)
            s_ref[g] = st
            s_hi = st.astype(BF16)
            s_lo = (st - s_hi.astype(F32)).astype(BF16)
            r = row(r_ref)
            r_hi = r.astype(BF16).astype(F32)
            lhs = jnp.where(row16 == sub, r_hi, jnp.where(row16 == 8 + sub, r - r_hi, 0.0)).astype(BF16)
            yy = (lax.dot_general(lhs, s_hi, last, preferred_element_type=F32)
                  + lax.dot_general(lhs, s_lo, last, preferred_element_type=F32))
            lanes = slice((g % HEADS) * HD, (g % HEADS + 1) * HD)
            y_ref[g // HEADS, pl.ds(base, 8), lanes] += yy[0:8] + yy[8:16]

    def two_steps(i, carry):
        t0 = 2 * i
        sel = jnp.where(sel_row == t0 + sel_blk, 1.0, 0.0).astype(BF16)
        for g in range(group):
            vbc_sc[g] = _dot(vsplit_sc[g], sel)
        one_step(t0, 0)
        one_step(t0 + 1, 1)
        return carry

    lax.fori_loop(0, steps // 2, two_steps, 0)


def wkv_scan(s0, r, w, k, a, b, vt, t_len, group):
    n = s0.shape[0]
    steps = min(t_len, SCAN_L)
    vec = pl.BlockSpec((group, steps, HD), lambda i, c: (i, c, 0))
    st = pl.BlockSpec((group, HD, HD), lambda i, c: (i, 0, 0))
    tr = pl.BlockSpec((group, HD, SCAN_L), lambda i, c: (i, 0, c))
    return pl.pallas_call(
        functools.partial(_scan_kernel, steps=steps, group=group),
        grid=(n // group, t_len // steps),
        in_specs=[st, vec, vec, vec, vec, vec, tr],
        out_specs=[st, pl.BlockSpec((group // HEADS, steps, RW), lambda i, c: (i, c, 0))],
        out_shape=[jax.ShapeDtypeStruct((n, HD, HD), F32), jax.ShapeDtypeStruct((n // HEADS, t_len, RW), F32)],
        scratch_shapes=[pltpu.VMEM((group, HD, 2 * SCAN_L), BF16), pltpu.VMEM((group, HD, 2 * LANES), F32)],
        compiler_params=_cparams(("arbitrary", "arbitrary")),
        name="wkv_scan",
    )(s0, r, w, k, a, b, vt)


def _post_kernel(y_ref, g_ref, bonus_ref, pp_ref, pool0_ref, x_ref, gate_ref, lnw_ref, lnb_ref, bd_ref,
                 wpool_ref, pscale_ref, wout_ref, o_ref, ext_sc, *, tt, pos0):
    t = pl.program_id(1)

    @pl.when(t == 0)
    def _():
        ext_sc[0:16, :] = pool0_ref[0]

    @pl.when(t > 0)
    def _():
        ext_sc[0:16, :] = ext_sc[tt:tt + 16, :]

    pp = pp_ref[0]
    ext_sc[16:16 + tt, :] = pp

    bd = bd_ref[...]
    y = y_ref[0]
    cen = y - _dot3(y, bd) * (1.0 / HD)
    var = _dot3(cen * cen, bd) * (1.0 / HD)
    yn = cen * lax.rsqrt(var + LN_X_EPS)
    y_rwkv = (yn * lnw_ref[...] + lnb_ref[...] + bonus_ref[0]) * g_ref[0]

    pos = pos0 + t * tt + lax.broadcasted_iota(jnp.int32, (tt, LANES), 0)
    out = _dot(y_rwkv.astype(BF16), wout_ref[0:RW, :])
    gd = PW // len(POOL_WINDOWS)
    for gi, win in enumerate(POOL_WINDOWS):
        lanes = slice(gi * gd, (gi + 1) * gd)
        acc = ext_sc[16:16 + tt, lanes]
        for j in range(1, win):
            acc = acc + ext_sc[16 - j:16 - j + tt, lanes]
        cnt = jnp.minimum(pos + 1, win).astype(F32)
        dlt = acc / cnt - pp[:, lanes]
        yp = _dot(dlt.astype(BF16), wpool_ref[gi]) * pscale_ref[:, lanes]
        out = out + _dot(yp.astype(BF16), wout_ref[RW + gi * gd:RW + (gi + 1) * gd, :])
    o_ref[0] = x_ref[0] + gate_ref[0] * out


def rwkv_post(y, g, bonus, pp, pool0, x, gate, p, tt, pos0):
    s, t_len, _ = pp.shape
    const = lambda shape: pl.BlockSpec(shape, lambda i, j: (0,) * len(shape))
    tok = lambda wd: pl.BlockSpec((1, tt, wd), lambda i, j: (i, j, 0))
    gd = PW // len(POOL_WINDOWS)
    return pl.pallas_call(
        functools.partial(_post_kernel, tt=tt, pos0=pos0),
        grid=(s, t_len // tt),
        in_specs=[tok(RW), tok(RW), tok(RW), tok(PW),
                  pl.BlockSpec((1, 16, PW), lambda i, j: (i, 0, 0)), tok(D),
                  pl.BlockSpec((1, 1, D), lambda i, j: (i, 0, 0)),
                  const((1, RW)), const((1, RW)), const((RW, RW)), const((len(POOL_WINDOWS), gd, gd)),
                  const((1, PW)), const((D, D))],
        out_specs=tok(D),
        out_shape=jax.ShapeDtypeStruct((s, t_len, D), F32),
        scratch_shapes=[pltpu.VMEM((16 + tt, PW), F32)],
        compiler_params=_cparams(("arbitrary", "arbitrary")),
        name="rwkv_post",
    )(y, g, bonus, pp, pool0, x, gate, p["ln_w"].reshape(1, RW), p["ln_b"].reshape(1, RW), p["bd"], p["w_pool"],
      p["pool_scale"].reshape(1, PW), p["w_out"])


def _sb_block(q_h, k_bf, v_bf, u_ref, carry_ref, acc_ref, h, valid):
    tk = k_bf.shape[0]
    logit = lax.dot_general(q_h, k_bf, (((1,), (1,)), ((), ())), preferred_element_type=F32)
    sp = _softplus(logit)
    log_1m = -sp
    if valid is not None:
        log_1m = jnp.where(valid, log_1m, 0.0)
    hi = log_1m.astype(BF16)
    lo = (log_1m - hi.astype(F32)).astype(BF16)
    u = u_ref[...]
    cs = _dot(hi, u) + _dot(lo, u)
    carry = carry_ref[h]
    carry = carry[:, :tk] if tk <= LANES else jnp.concatenate([carry] * (tk // LANES), axis=1)
    wts = jnp.exp(logit - sp + cs[:, :tk] + carry)
    if valid is not None:
        wts = jnp.where(valid, wts, 0.0)
    acc_ref[h] += _dot(wts.astype(BF16), v_bf)
    carry_ref[h] += cs[:, tk:]


def _sb_kernel(*refs, tq, tk, n_cache, tkc, pairs):
    if n_cache:
        q_ref, kn_ref, vn_ref, un_ref, kc_ref, vc_ref, uc_ref, o_ref, carry_ref, acc_ref = refs
    else:
        q_ref, kn_ref, vn_ref, un_ref, o_ref, carry_ref, acc_ref = refs
    qi = pl.program_id(2)
    m = tq // tk
    lane = lax.broadcasted_iota(jnp.int32, (tq, LANES), 1)
    q_heads = []
    for p in range(pairs):
        q2 = q_ref[0, :, p * LANES:(p + 1) * LANES] * (HD ** -0.5)
        q_heads += [jnp.where(lane < HD, q2, 0.0).astype(BF16), jnp.where(lane >= HD, q2, 0.0).astype(BF16)]
    carry_ref[...] = jnp.zeros_like(carry_ref)
    acc_ref[...] = jnp.zeros_like(acc_ref)

    def visit(k_ref, v_ref, start, rows, u_ref, valid):
        for p in range(pairs):
            kb = k_ref[0, pl.ds(start, rows), p * LANES:(p + 1) * LANES].astype(BF16)
            vb = v_ref[0, pl.ds(start, rows), p * LANES:(p + 1) * LANES].astype(BF16)
            for h in range(2):
                _sb_block(q_heads[2 * p + h], kb, vb, u_ref, carry_ref, acc_ref, 2 * p + h, valid)

    qpos = qi * tq + lax.broadcasted_iota(jnp.int32, (tq, tk), 0)
    col = lax.broadcasted_iota(jnp.int32, (tq, tk), 1)
    for i in range(m):
        start = pl.multiple_of((qi * m + (m - 1 - i)) * tk, tk)
        visit(kn_ref, vn_ref, start, tk, un_ref, start + col < qpos)

    def earlier(it, c):
        visit(kn_ref, vn_ref, pl.multiple_of((qi * m - 1 - it) * tk, tk), tk, un_ref, None)
        return c

    lax.fori_loop(0, qi * m, earlier, 0)

    if n_cache:
        def cached(it, c):
            visit(kc_ref, vc_ref, pl.multiple_of((n_cache - 1 - it) * tkc, tkc), tkc, uc_ref, None)
            return c

        lax.fori_loop(0, n_cache, cached, 0)

    for p in range(pairs):
        o_ref[0, :, p * LANES:(p + 1) * LANES] = jnp.where(lane < HD, acc_ref[2 * p], acc_ref[2 * p + 1])


def _cumsum_matrix(tk):
    j = np.arange(tk)[:, None]
    s = np.arange(tk)[None, :]
    return jnp.asarray(np.concatenate([(j > s), np.ones((tk, LANES), bool)], axis=1), BF16)


def stick_breaking(q, k, v, ck, cv, tq, tk, tkc, pairs):
    b, t_len, _ = q.shape
    n_cache = 0 if ck is None else ck.shape[1] // tkc
    wd = pairs * LANES
    cols = lambda rows: pl.BlockSpec((1, rows, wd), lambda i, hp, qi: (i, 0, hp))
    const = lambda shape: pl.BlockSpec(shape, lambda i, hp, qi: (0, 0))
    qspec = pl.BlockSpec((1, tq, wd), lambda i, hp, qi: (i, qi, hp))
    in_specs = [qspec, cols(t_len), cols(t_len), const((tk, tk + LANES))]
    args = [q, k, v, _cumsum_matrix(tk)]
    if n_cache:
        in_specs += [cols(ck.shape[1]), cols(ck.shape[1]), const((tkc, tkc + LANES))]
        args += [ck, cv, _cumsum_matrix(tkc)]
    return pl.pallas_call(
        functools.partial(_sb_kernel, tq=tq, tk=tk, n_cache=n_cache, tkc=tkc, pairs=pairs),
        grid=(b, D // wd, t_len // tq),
        in_specs=in_specs,
        out_specs=qspec,
        out_shape=jax.ShapeDtypeStruct((b, t_len, D), F32),
        scratch_shapes=[pltpu.VMEM((2 * pairs, tq, LANES), F32), pltpu.VMEM((2 * pairs, tq, LANES), F32)],
        compiler_params=_cparams(("arbitrary", "arbitrary", "arbitrary")),
        name="stick_breaking",
    )(*args)


def _proj_res_kernel(a_ref, w_ref, x_ref, gate_ref, o_ref):
    y = _dot(a_ref[...].astype(BF16), w_ref[...])
    for q in range(TM // CHUNK):
        rows = slice(q * CHUNK, (q + 1) * CHUNK)
        o_ref[rows, :] = x_ref[rows, :] + gate_ref[0, q:q + 1, :] * y[rows, :]


def proj_residual(a, w_bf, x, gate):
    nt = x.shape[0]
    q = TM // CHUNK
    return pl.pallas_call(
        _proj_res_kernel,
        grid=(nt // TM,),
        in_specs=[pl.BlockSpec((TM, D), lambda i: (i, 0)), pl.BlockSpec((D, D), lambda i: (0, 0)),
                  pl.BlockSpec((TM, D), lambda i: (i, 0)), pl.BlockSpec((1, q, D), lambda i: (i, 0, 0))],
        out_specs=pl.BlockSpec((TM, D), lambda i: (i, 0)),
        out_shape=jax.ShapeDtypeStruct((nt, D), F32),
        compiler_params=_cparams(("arbitrary",)),
        name="proj_residual",
    )(a, w_bf, x, gate)


def _route_kernel(x_ref, g_ref, sh_ref, sc_ref, wr_ref, br_ref, tri_ref, h_o, mi_o, mf_o, cnt_o, carry_sc):
    i = pl.program_id(0)

    @pl.when(i == 0)
    def _():
        carry_sc[...] = jnp.zeros_like(carry_sc)

    _mod_tile(x_ref, g_ref, sh_ref, sc_ref, h_o)
    h = h_o[...]
    h_hi = h.astype(BF16)
    h_lo = (h - h_hi.astype(F32)).astype(BF16)
    w_hi = wr_ref[0]
    w_lo = wr_ref[1]
    lg = _dot(h_hi, w_hi) + _dot(h_hi, w_lo) + _dot(h_lo, w_hi) + br_ref[...]
    lane = lax.broadcasted_iota(jnp.int32, lg.shape, 1)
    lane_f = lane.astype(F32)
    neg = jnp.float32(-3.0e38)
    vals, idxs = [], []
    mask = jnp.zeros(lg.shape, F32)
    for _ in range(TOP_K):
        m = jnp.max(lg, axis=1, keepdims=True)
        idx = jnp.min(jnp.where(lg == m, lane_f, float(LANES)), axis=1, keepdims=True)
        sel = lane_f == idx
        mask = jnp.where(sel, 1.0, mask)
        lg = jnp.where(sel, neg, lg)
        vals.append(m)
        idxs.append(idx)
    es = [jnp.exp(vj - vals[0]) for vj in vals]
    den = es[0] + es[1] + es[2] + es[3]
    rank = _dot(tri_ref[...], mask.astype(BF16)) + carry_sc[...]
    carry_sc[...] += jnp.sum(mask, axis=0, keepdims=True)
    mi = jnp.zeros(lg.shape, F32)
    mf = jnp.zeros(lg.shape, F32)
    for j in range(TOP_K):
        rj = jnp.sum(jnp.where(lane_f == idxs[j], rank, 0.0), axis=1, keepdims=True)
        mi = jnp.where(lane == j, idxs[j], mi)
        mi = jnp.where(lane == TOP_K + j, rj, mi)
        mf = jnp.where(lane == j, es[j] / den, mf)
    mi_o[...] = mi.astype(jnp.int32)
    mf_o[...] = mf
    cnt_o[...] = jnp.broadcast_to(carry_sc[...], cnt_o.shape)


def moe_route(x, g, sh, sc, wr2, br, tri):
    nt = x.shape[0]
    q = TM // CHUNK
    return pl.pallas_call(
        _route_kernel,
        grid=(nt // TM,),
        in_specs=[pl.BlockSpec((TM, D), lambda i: (i, 0)), pl.BlockSpec((1, D), lambda i: (0, 0)),
                  pl.BlockSpec((1, q, D), lambda i: (i, 0, 0)), pl.BlockSpec((1, q, D), lambda i: (i, 0, 0)),
                  pl.BlockSpec((2, D, LANES), lambda i: (0, 0, 0)), pl.BlockSpec((1, LANES), lambda i: (0, 0)),
                  pl.BlockSpec((TM, TM), lambda i: (0, 0))],
        out_specs=[pl.BlockSpec((TM, D), lambda i: (i, 0)), pl.BlockSpec((TM, LANES), lambda i: (i, 0)),
                   pl.BlockSpec((TM, LANES), lambda i: (i, 0)), pl.BlockSpec((8, LANES), lambda i: (0, 0))],
        out_shape=[jax.ShapeDtypeStruct((nt, D), F32), jax.ShapeDtypeStruct((nt, LANES), jnp.int32),
                   jax.ShapeDtypeStruct((nt, LANES), F32), jax.ShapeDtypeStruct((8, LANES), F32)],
        scratch_shapes=[pltpu.VMEM((1, LANES), F32)],
        compiler_params=_cparams(("arbitrary",)),
        name="moe_route",
    )(x, g, sh, sc, wr2, br, tri)


def _dispatch_kernel(dest_ref, h_ref, xs_in, xs_out, sem):
    del xs_in

    def copy(r, j):
        return pltpu.make_async_copy(h_ref.at[pl.ds(r, 1)], xs_out.at[pl.ds(dest_ref[r * TOP_K + j], 1)], sem)

    def issue(r, c):
        for j in range(TOP_K):
            copy(r, j).start()
        return c

    def drain(r, c):
        for j in range(TOP_K):
            copy(r, j).wait()
        return c

    lax.fori_loop(0, TM, issue, 0)
    lax.fori_loop(0, TM, drain, 0)


def moe_dispatch(dest_flat, h, xs_zero):
    nt = h.shape[0]
    return pl.pallas_call(
        _dispatch_kernel,
        grid=(nt // TM,),
        in_specs=[pl.BlockSpec((TM * TOP_K,), lambda i: (i,), memory_space=pltpu.SMEM),
                  pl.BlockSpec((TM, D), lambda i: (i, 0)),
                  pl.BlockSpec(memory_space=pl.ANY)],
        out_specs=pl.BlockSpec(memory_space=pl.ANY),
        out_shape=jax.ShapeDtypeStruct(xs_zero.shape, xs_zero.dtype),
        scratch_shapes=[pltpu.SemaphoreType.DMA(())],
        input_output_aliases={2: 0},
        compiler_params=_cparams(("arbitrary",)),
        name="moe_dispatch",
    )(dest_flat, h, xs_zero)


def _expert_kernel(te_ref, nu_ref, x_ref, wu_ref, bu_ref, wd_ref, bd_ref, o_ref):
    i = pl.program_id(0)

    @pl.when(i < nu_ref[0])
    def _():
        x = x_ref[...].astype(BF16)
        u = _dot(x, wu_ref[0]) + bu_ref[0]
        nxt = pltpu.roll(u, u.shape[1] - 1, axis=1)
        glu = jnp.minimum(u, SWIGLU_LIMIT)
        lin = jnp.clip(nxt, -SWIGLU_LIMIT, SWIGLU_LIMIT)
        act = glu * jax.nn.sigmoid(SWIGLU_ALPHA * glu) * (lin + 1.0)
        lane = lax.broadcasted_iota(jnp.int32, act.shape, 1)
        act = jnp.where(lane % 2 == 0, act, 0.0)
        o_ref[...] = _dot(act.astype(BF16), wd_ref[0]) + bd_ref[0]

    @pl.when(i >= nu_ref[0])
    def _():
        o_ref[...] = jnp.zeros_like(o_ref)


def moe_experts(tile_expert, n_used, xs, wu, bu, wd, bd):
    rows = xs.shape[0]
    xspec = pl.BlockSpec((TME, D), lambda i, te, nu: (i, 0))
    return pl.pallas_call(
        _expert_kernel,
        grid_spec=pltpu.PrefetchScalarGridSpec(
            num_scalar_prefetch=2, grid=(rows // TME,),
            in_specs=[xspec,
                      pl.BlockSpec((1, D, 2 * D), lambda i, te, nu: (te[i], 0, 0)),
                      pl.BlockSpec((1, 1, 2 * D), lambda i, te, nu: (te[i], 0, 0)),
                      pl.BlockSpec((1, 2 * D, D), lambda i, te, nu: (te[i], 0, 0)),
                      pl.BlockSpec((1, 1, D), lambda i, te, nu: (te[i], 0, 0))],
            out_specs=xspec),
        out_shape=jax.ShapeDtypeStruct((rows, D), F32),
        compiler_params=_cparams(("arbitrary",)),
        name="moe_experts",
    )(tile_expert, n_used, xs, wu, bu, wd, bd)


def _combine_kernel(dest_ref, x_ref, mf_ref, gate_ref, ys_ref, o_ref, buf, sem):
    def copy(r, j):
        return pltpu.make_async_copy(ys_ref.at[pl.ds(dest_ref[r * TOP_K + j], 1)], buf.at[j, pl.ds(r, 1)], sem)

    def issue(r, c):
        for j in range(TOP_K):
            copy(r, j).start()
        return c

    def drain(r, c):
        for j in range(TOP_K):
            copy(r, j).wait()
        return c

    lax.fori_loop(0, TM, issue, 0)
    lax.fori_loop(0, TM, drain, 0)
    mf = mf_ref[...]
    y = buf[0] * mf[:, 0:1]
    for j in range(1, TOP_K):
        y = y + buf[j] * mf[:, j:j + 1]
    for q in range(TM // CHUNK):
        rows = slice(q * CHUNK, (q + 1) * CHUNK)
        o_ref[rows, :] = x_ref[rows, :] + gate_ref[0, q:q + 1, :] * y[rows, :]


def moe_combine(dest_flat, x, mf, gate, ys):
    nt = x.shape[0]
    q = TM // CHUNK
    return pl.pallas_call(
        _combine_kernel,
        grid=(nt // TM,),
        in_specs=[pl.BlockSpec((TM * TOP_K,), lambda i: (i,), memory_space=pltpu.SMEM),
                  pl.BlockSpec((TM, D), lambda i: (i, 0)), pl.BlockSpec((TM, LANES), lambda i: (i, 0)),
                  pl.BlockSpec((1, q, D), lambda i: (i, 0, 0)), pl.BlockSpec(memory_space=pl.ANY)],
        out_specs=pl.BlockSpec((TM, D), lambda i: (i, 0)),
        out_shape=jax.ShapeDtypeStruct((nt, D), F32),
        scratch_shapes=[pltpu.VMEM((TOP_K, TM, D), F32), pltpu.SemaphoreType.DMA(())],
        compiler_params=_cparams(("arbitrary",)),
        name="moe_combine",
    )(dest_flat, x, mf, gate, ys)


def moe_layer(x, g, sh, sc, gate, mp):
    nt = x.shape[0]
    h, mi, mf, cnt = moe_route(x, g, sh, sc, mp["wr2"], mp["br"], mp["tri"])
    counts = cnt[0, :N_EXPERTS].astype(jnp.int32)
    padded = (counts + TME - 1) // TME * TME
    pad_end = jnp.cumsum(padded)
    pad_start = pad_end - padded
    dest = pad_start[mi[:, :TOP_K]] + mi[:, TOP_K:2 * TOP_K]
    dest_flat = dest.reshape(-1).astype(jnp.int32)
    n_tiles = (nt * TOP_K + N_EXPERTS * (TME - 1)) // TME
    tile_start = jnp.arange(n_tiles, dtype=jnp.int32) * TME
    tile_expert = jnp.minimum(jnp.sum(pad_end[None, :] <= tile_start[:, None], axis=1), N_EXPERTS - 1).astype(jnp.int32)
    n_used = (pad_end[-1:] // TME).astype(jnp.int32)
    xs = moe_dispatch(dest_flat, h, jnp.zeros((n_tiles * TME, D), F32))
    ys = moe_experts(tile_expert, n_used, xs, mp["wu"], mp["bu"], mp["wd"], mp["bd"])
    return moe_combine(dest_flat, x, mf, gate, ys)


def _final_kernel(x_ref, g_ref, o_ref):
    x = x_ref[...]
    ms = jnp.mean(x * x, axis=-1, keepdims=True)
    o_ref[...] = x * lax.rsqrt(ms + EPS) * g_ref[...]


def final_norm(x, g):
    nt = x.shape[0]
    return pl.pallas_call(
        _final_kernel,
        grid=(nt // TM,),
        in_specs=[pl.BlockSpec((TM, D), lambda i: (i, 0)), pl.BlockSpec((1, D), lambda i: (0, 0))],
        out_specs=pl.BlockSpec((TM, D), lambda i: (i, 0)),
        out_shape=jax.ShapeDtypeStruct((nt, D), F32),
        compiler_params=_cparams(("arbitrary",)),
        name="final_norm",
    )(x, g.reshape(1, D))


def rwkv_pool_layer(x, mods, g, lp, groups, state):
    sh, sc, gate_rows = mods
    ps, pp = normmm(x, g, sh, sc, lp["w_in"], (SHIFT_W, PW))
    outs, states = [], []
    for (r0, s, t_len, tt, pos0, group), (wkv0, shift0, pool0) in zip(groups, state):
        n = s * t_len
        ps_g = ps[r0:r0 + n].reshape(s, t_len, SHIFT_W)
        pp_g = pp[r0:r0 + n].reshape(s, t_len, PW)
        r, w, k, a, b, vt, gt, bonus = rwkv_prep(ps_g, shift0, lp, tt)
        flat = lambda z: z.reshape(s * HEADS, t_len, HD)
        s_new, y = wkv_scan(wkv0.reshape(s * HEADS, HD, HD), flat(r), flat(w), flat(k), flat(a), flat(b),
                            vt.reshape(s * HEADS, HD, vt.shape[2]), t_len, group)
        pool_pad = jnp.concatenate([jnp.zeros((s, 1, PW), F32), pool0], axis=1)
        x_g = x[r0:r0 + n].reshape(s, t_len, D)
        out = rwkv_post(y, gt, bonus, pp_g, pool_pad, x_g,
                        gate_rows[r0 // CHUNK:(r0 + n) // CHUNK:t_len // CHUNK].reshape(s, 1, D), lp, tt, pos0)
        outs.append(out.reshape(n, D))
        states.append((s_new.reshape(s, HEADS, HD, HD), ps_g[:, -1], pp_g[:, t_len - POOL_HIST:]))
    return jnp.concatenate(outs, axis=0), states


def attention_layer(x, mods, g, lp, groups, caches):
    sh, sc, gate = mods
    q, k, v = normmm(x, g, sh, sc, lp["w_qkv"], (D, D, D))
    outs, kvs = [], []
    for (r0, s, t_len, tq, tk, tkc, pairs), cache in zip(groups, caches):
        n = s * t_len
        seq = lambda z: z[r0:r0 + n].reshape(s, t_len, D)
        ck, cv = cache
        outs.append(stick_breaking(seq(q), seq(k), seq(v), ck, cv, tq, tk, tkc, pairs).reshape(n, D))
        kvs.append((seq(k).reshape(s, t_len, D // HD, HD), seq(v).reshape(s, t_len, D // HD, HD)))
    o = jnp.concatenate(outs, axis=0)
    return proj_residual(o, lp["w_out"], x, gate), kvs


def kernel(x_prompt, x_sample, c_prompt, c_sample, state_wkv, state_shift, state_pool, cache_k, cache_v, ada_w, ada_b, norm_mix_g, norm_ffn_g, final_g, a_w_in, a_mu, a_w0, a_w_decay, a_a0, a_w_iclr, a_w_gate, a_k_k, a_k_a, a_r_k, a_ln_w, a_ln_b, b_w_pool, b_pool_scale, ab_w_out, c_w_qkv, c_w_out, moe_w_router, moe_b_router, moe_w_up, moe_b_up, moe_w_down, moe_b_down):
    bp, tp, _ = x_prompt.shape
    bs, ts, _ = x_sample.shape
    past = cache_k.shape[2]
    n_p, n_s = bp * tp, bs * ts
    nt = n_p + n_s
    x = jnp.concatenate([x_prompt.reshape(n_p, D), x_sample.reshape(n_s, D)], axis=0)

    mod = ada_mod(jnp.concatenate([c_prompt, c_sample], axis=0), ada_w, ada_b)
    seq_of_chunk = np.concatenate([np.repeat(np.arange(bp), tp // CHUNK), bp + np.repeat(np.arange(bs), ts // CHUNK)])
    q = TM // CHUNK

    def mod_rows(layer, idx):
        rows = mod[layer, :, idx * D:(idx + 1) * D][seq_of_chunk]
        return rows, rows.reshape(nt // TM, q, D)

    zeros = lambda *shape: jnp.zeros(shape, F32)
    bd = jnp.asarray(np.kron(np.eye(HEADS), np.ones((HD, HD))), BF16)
    tri = jnp.asarray(np.tril(np.ones((TM, TM)), -1), BF16)

    def moe_params(layer):
        wr = jnp.pad(moe_w_router[layer], ((0, 0), (0, LANES - N_EXPERTS)))
        wr_hi = wr.astype(BF16)
        wr_lo = (wr - wr_hi.astype(F32)).astype(BF16)
        br = jnp.concatenate([moe_b_router[layer], jnp.full((LANES - N_EXPERTS,), -1e30, F32)]).reshape(1, LANES)
        wd = moe_w_down[layer].astype(BF16)
        wd = jnp.stack([wd, jnp.zeros_like(wd)], axis=2).reshape(N_EXPERTS, 2 * D, D)
        return dict(wr2=jnp.stack([wr_hi, wr_lo]), br=br, tri=tri,
                    wu=moe_w_up[layer].astype(BF16), bu=moe_b_up[layer][:, None, :],
                    wd=wd, bd=moe_b_down[layer][:, None, :])

    w_lora = jnp.zeros((128, 2 * RW), F32).at[:64, :RW].set(a_w_decay[0]).at[64:, RW:].set(a_w_iclr[0])
    lp0 = dict(w_in=a_w_in[0].astype(BF16), mu=a_mu[0], w0=a_w0[0], w_lora=w_lora.astype(BF16), a0=a_a0[0],
               w_gate=a_w_gate[0].astype(BF16), k_k=a_k_k[0], k_a=a_k_a[0], r_k=a_r_k[0], bd=bd,
               ln_w=a_ln_w[0], ln_b=a_ln_b[0], w_pool=b_w_pool[0].astype(BF16), pool_scale=b_pool_scale[0],
               w_out=ab_w_out[0].astype(BF16))
    lp1 = dict(w_qkv=c_w_qkv[0].astype(BF16), w_out=c_w_out[0].astype(BF16))

    sh1, sc1, g1 = (mod_rows(0, i) for i in (0, 1, 2))
    groups0 = [(0, bp, tp, 256, 0, 16), (n_p, bs, ts, ts, past, 16)]
    state0 = [(zeros(bp, HEADS, HD, HD), zeros(bp, SHIFT_W), zeros(bp, POOL_HIST, PW)),
              (state_wkv[0], state_shift[0], state_pool[0])]
    x, st0 = rwkv_pool_layer(x, (sh1[1], sc1[1], g1[0]), norm_mix_g[0].reshape(1, D), lp0, groups0, state0)
    sh2, sc2, g2 = (mod_rows(0, i) for i in (3, 4, 5))
    x = moe_layer(x, norm_ffn_g[0].reshape(1, D), sh2[1], sc2[1], g2[1], moe_params(0))

    sh1, sc1, g1 = (mod_rows(1, i) for i in (0, 1, 2))
    groups1 = [(0, bp, tp, 512, 256, 256, 1), (n_p, bs, ts, ts, ts, 256, 4)]
    caches = [(None, None), (cache_k[0].reshape(bs, past, D), cache_v[0].reshape(bs, past, D))]
    x, kv1 = attention_layer(x, (sh1[1], sc1[1], g1[1]), norm_mix_g[1].reshape(1, D), lp1, groups1, caches)
    sh2, sc2, g2 = (mod_rows(1, i) for i in (3, 4, 5))
    x = moe_layer(x, norm_ffn_g[1].reshape(1, D), sh2[1], sc2[1], g2[1], moe_params(1))

    y = final_norm(x, final_g)
    lead = lambda z: z[None]
    (p_wkv, p_shift, p_pool), (s_wkv, s_shift, s_pool) = st0
    (p_k, p_v), (s_k, s_v) = kv1
    return (y[:n_p].reshape(bp, tp, D), y[n_p:].reshape(bs, ts, D),
            lead(p_wkv), lead(p_shift), lead(p_pool), lead(p_k), lead(p_v),
            lead(s_wkv), lead(s_shift), lead(s_pool), lead(s_k), lead(s_v))
```

```python
import functools

import jax
import jax.numpy as jnp
import numpy as np
from jax import lax
from jax.experimental import pallas as pl
from jax.experimental.pallas import tpu as pltpu

F32 = jnp.float32
BF16 = jnp.bfloat16

D = 1024
CHUNK = 64
HEADS = 8
HD = 64
RW = HEADS * HD
PW = D - RW
POOL_WINDOWS = (2, 4, 8, 16)
POOL_HIST = 15
SHIFT_W = 3 * RW + 64 + 64 + 128
N_EXPERTS = 32
TOP_K = 4
EPS = 1e-6
LN_X_EPS = 64e-5
SWIGLU_LIMIT = 7.0
SWIGLU_ALPHA = 1.702
LOG2_E = 1.4426950408889634

LANES = 128
TM = 256
TME = 256
SCAN_L = 128
VMEM_LIMIT = 56 * 1024 * 1024


def _cparams(sem):
    return pltpu.CompilerParams(dimension_semantics=sem, vmem_limit_bytes=VMEM_LIMIT)


def _softplus(z):
    return jnp.maximum(z, 0.0) + jnp.log(1.0 + jnp.exp(-jnp.abs(z)))


def _split3(x):
    hi = x.astype(BF16)
    r1 = x - hi.astype(F32)
    mid = r1.astype(BF16)
    lo = (r1 - mid.astype(F32)).astype(BF16)
    return hi, mid, lo


def _dot(a, b):
    return jnp.dot(a, b, preferred_element_type=F32)


def _dot3(x, w01):
    hi, mid, lo = _split3(x)
    return _dot(hi, w01) + _dot(mid, w01) + _dot(lo, w01)


def _mod_tile(x_ref, g_ref, sh_ref, sc_ref, h_ref):
    g = g_ref[...]
    for q in range(x_ref.shape[0] // CHUNK):
        x = x_ref[q * CHUNK:(q + 1) * CHUNK, :]
        ms = jnp.mean(x * x, axis=-1, keepdims=True)
        y = x * lax.rsqrt(ms + EPS) * g
        h = y * (1.0 + sc_ref[0, q:q + 1, :]) + sh_ref[0, q:q + 1, :]
        h_ref[q * CHUNK:(q + 1) * CHUNK, :] = h.astype(h_ref.dtype)


def _ada_kernel(c_ref, w_ref, b_ref, o_ref):
    c = c_ref[...]
    s = c * jax.nn.sigmoid(c)
    o_ref[0] = _dot(s.astype(BF16), w_ref[0].astype(BF16)) + b_ref[0]


def ada_mod(c_all, ada_w, ada_b):
    depth, _, n = ada_w.shape
    s = c_all.shape[0]
    tn = 1536
    return pl.pallas_call(
        _ada_kernel,
        grid=(depth, n // tn),
        in_specs=[pl.BlockSpec((s, D), lambda l, j: (0, 0)),
                  pl.BlockSpec((1, D, tn), lambda l, j: (l, 0, j)),
                  pl.BlockSpec((1, 1, tn), lambda l, j: (l, 0, j))],
        out_specs=pl.BlockSpec((1, s, tn), lambda l, j: (l, 0, j)),
        out_shape=jax.ShapeDtypeStruct((depth, s, n), F32),
        compiler_params=_cparams(("arbitrary", "arbitrary")),
        name="ada_mod",
    )(c_all, ada_w, ada_b.reshape(depth, 1, n))


def _normmm_kernel(x_ref, g_ref, sh_ref, sc_ref, w_ref, *rest, widths):
    outs, h_ref = rest[:-1], rest[-1]
    _mod_tile(x_ref, g_ref, sh_ref, sc_ref, h_ref)
    h = h_ref[...]
    off = 0
    for o_ref, wd in zip(outs, widths):
        o_ref[...] = _dot(h, w_ref[:, off:off + wd])
        off += wd


def normmm(x, g, sh, sc, w_bf, widths):
    nt = x.shape[0]
    n = w_bf.shape[1]
    q = TM // CHUNK
    return pl.pallas_call(
        functools.partial(_normmm_kernel, widths=widths),
        grid=(nt // TM,),
        in_specs=[pl.BlockSpec((TM, D), lambda i: (i, 0)),
                  pl.BlockSpec((1, D), lambda i: (0, 0)),
                  pl.BlockSpec((1, q, D), lambda i: (i, 0, 0)),
                  pl.BlockSpec((1, q, D), lambda i: (i, 0, 0)),
                  pl.BlockSpec((D, n), lambda i: (0, 0))],
        out_specs=[pl.BlockSpec((TM, wd), lambda i: (i, 0)) for wd in widths],
        out_shape=[jax.ShapeDtypeStruct((nt, wd), F32) for wd in widths],
        scratch_shapes=[pltpu.VMEM((TM, D), BF16)],
        compiler_params=_cparams(("arbitrary",)),
        name="normmm",
    )(x, g, sh, sc, w_bf)


def _t_pad(x, rows):
    m = x.shape[0]
    if m < rows:
        x = jnp.concatenate([x, jnp.zeros((rows - m, x.shape[1]), x.dtype)], axis=0)
    return x.T


def _prep_kernel(ps_ref, s0_ref, mu_ref, w0_ref, wlora_ref, a0_ref, wgate_ref, kk_ref, ka_ref, rk_ref,
                 bd_ref, r_o, w_o, k_o, a_o, b_o, vt_o, g_o, bonus_o, prev_sc, *, tt, ttp):
    t = pl.program_id(1)

    @pl.when(t == 0)
    def _():
        prev_sc[...] = s0_ref[0]

    ps = ps_ref[0]
    rolled = pltpu.roll(ps, 1, axis=0)
    row = lax.broadcasted_iota(jnp.int32, ps.shape, 0)
    prev = jnp.where(row == 0, prev_sc[...], rolled)
    prev_sc[...] = ps[tt - 1:tt, :]
    xs = ps + (prev - ps) * mu_ref[...]
    r = xs[:, 0:RW]
    k = xs[:, RW:2 * RW]
    v = xs[:, 2 * RW:3 * RW]
    xwa = xs[:, 3 * RW:3 * RW + 128]
    xg = xs[:, 3 * RW + 128:3 * RW + 256]
    lane = lax.broadcasted_iota(jnp.int32, xwa.shape, 1)
    lin = jnp.where(lane < 64, jnp.tanh(xwa), xwa)
    lora = _dot(lin.astype(BF16), wlora_ref[...])
    log_w = -_softplus(-(w0_ref[...] + lora[:, :RW])) - 0.5
    decay = jnp.exp(-jnp.exp(log_w))
    a = jax.nn.sigmoid(a0_ref[...] + lora[:, RW:])
    g_o[0] = _dot(jax.nn.sigmoid(xg).astype(BF16), wgate_ref[...])
    bd = bd_ref[...]
    kk = k * kk_ref[...]
    n2 = _dot3(kk * kk, bd)
    kk = kk * lax.rsqrt(jnp.maximum(n2, 1e-24))
    k_mod = k * (1.0 + (a - 1.0) * ka_ref[...])
    bonus_o[0] = _dot3(r * k_mod * rk_ref[...], bd) * v
    a_vec = -kk
    b_vec = kk * a
    for h in range(HEADS):
        sl = slice(h * HD, (h + 1) * HD)
        r_o[0, h] = r[:, sl]
        w_o[0, h] = decay[:, sl]
        k_o[0, h] = k_mod[:, sl]
        a_o[0, h] = a_vec[:, sl]
        b_o[0, h] = b_vec[:, sl]
    vt_o[0] = _t_pad(v, ttp)


def rwkv_prep(ps, shift0, p, tt):
    s, t_len, _ = ps.shape
    ttp = max(tt, LANES)
    tpad = (t_len // tt) * ttp
    row = lambda a: a.reshape(1, -1)
    const = lambda shape: pl.BlockSpec(shape, lambda i, j: (0,) * len(shape))
    head_spec = pl.BlockSpec((1, HEADS, tt, HD), lambda i, j: (i, 0, j, 0))
    head_shape = jax.ShapeDtypeStruct((s, HEADS, t_len, HD), F32)
    tok_spec = pl.BlockSpec((1, tt, RW), lambda i, j: (i, j, 0))
    tok_shape = jax.ShapeDtypeStruct((s, t_len, RW), F32)
    return pl.pallas_call(
        functools.partial(_prep_kernel, tt=tt, ttp=ttp),
        grid=(s, t_len // tt),
        in_specs=[pl.BlockSpec((1, tt, SHIFT_W), lambda i, j: (i, j, 0)),
                  pl.BlockSpec((1, 1, SHIFT_W), lambda i, j: (i, 0, 0)),
                  const((1, SHIFT_W)), const((1, RW)), const((128, 2 * RW)), const((1, RW)),
                  const((128, RW)), const((1, RW)), const((1, RW)), const((1, RW)), const((RW, RW))],
        out_specs=[head_spec] * 5 + [pl.BlockSpec((1, RW, ttp), lambda i, j: (i, 0, j)), tok_spec, tok_spec],
        out_shape=[head_shape] * 5 + [jax.ShapeDtypeStruct((s, RW, tpad), F32), tok_shape, tok_shape],
        scratch_shapes=[pltpu.VMEM((1, SHIFT_W), F32)],
        compiler_params=_cparams(("arbitrary", "arbitrary")),
        name="rwkv_prep",
    )(ps, shift0.reshape(s, 1, SHIFT_W), row(p["mu"]), row(p["w0"]), p["w_lora"], row(p["a0"]),
      p["w_gate"], row(p["k_k"]), row(p["k_a"]), row(p["r_k"]), p["bd"])


def _scan_kernel(s0_ref, r_ref, w_ref, k_ref, a_ref, b_ref, vt_ref, s_ref, y_ref, vsplit_sc, vbc_a, vbc_b, *y_scs,
                 steps, group):
    c = pl.program_id(1)

    @pl.when(c == 0)
    def _():
        s_ref[...] = s0_ref[...]

    for g in range(group):
        vt = vt_ref[g]
        hi = vt.astype(BF16)
        vsplit_sc[g, :, 0:SCAN_L] = hi
        vsplit_sc[g, :, SCAN_L:2 * SCAN_L] = (vt - hi.astype(F32)).astype(BF16)
        y_scs[g][...] = jnp.zeros_like(y_scs[g])
    sel_row = jnp.bitwise_and(lax.broadcasted_iota(jnp.int32, (2 * SCAN_L, 2 * LANES), 0), SCAN_L - 1)
    sel_blk = lax.broadcasted_iota(jnp.int32, (2 * SCAN_L, 2 * LANES), 1) // LANES
    row16 = lax.broadcasted_iota(jnp.int32, (16, HD), 0)
    last = (((1,), (1,)), ((), ()))

    def select_columns(t0, vbc):
        sel = jnp.where(sel_row == t0 + sel_blk, 1.0, 0.0).astype(BF16)
        for g in range(group):
            vbc[g] = _dot(vsplit_sc[g], sel)

    def one_step(t, vbc, u):
        sub = jnp.bitwise_and(t, 7)
        base = pl.multiple_of(t - sub, 8)
        for g in range(group):
            row = lambda ref: ref[g, pl.ds(t, 1), :]
            st = s_ref[g]
            sa = jnp.sum(st * row(a_ref), axis=1, keepdims=True)
            st = st * row(w_ref) + sa * row(b_ref) + vbc[g][:, u * LANES:u * LANES + HD] * row(k_ref)
            s_ref[g] = st
            r = row(r_ref)
            r_hi = r.astype(BF16).astype(F32)
            lhs = jnp.where(row16 == sub, r_hi, jnp.where(row16 == 8 + sub, r - r_hi, 0.0)).astype(BF16)
            yy = lax.dot_general(lhs, st.astype(BF16), last, preferred_element_type=F32)
            y_scs[g][pl.ds(base, 8), :] += yy[0:8] + yy[8:16]

    def four_steps(i, carry):
        t0 = 4 * i
        select_columns(t0 + 2, vbc_b)
        one_step(t0, vbc_a, 0)
        one_step(t0 + 1, vbc_a, 1)
        select_columns(t0 + 4, vbc_a)
        one_step(t0 + 2, vbc_b, 0)
        one_step(t0 + 3, vbc_b, 1)
        return carry

    select_columns(0, vbc_a)
    lax.fori_loop(0, steps // 4, four_steps, 0)
    for g in range(group):
        y_ref[g // HEADS, :, (g % HEADS) * HD:(g % HEADS + 1) * HD] = y_scs[g][...]


def wkv_scan(s0, r, w, k, a, b, vt, t_len, group):
    n = s0.shape[0]
    steps = min(t_len, SCAN_L)
    vec = pl.BlockSpec((group, steps, HD), lambda i, c: (i, c, 0))
    st = pl.BlockSpec((group, HD, HD), lambda i, c: (i, 0, 0))
    tr = pl.BlockSpec((group, HD, SCAN_L), lambda i, c: (i, 0, c))
    return pl.pallas_call(
        functools.partial(_scan_kernel, steps=steps, group=group),
        grid=(n // group, t_len // steps),
        in_specs=[st, vec, vec, vec, vec, vec, tr],
        out_specs=[st, pl.BlockSpec((group // HEADS, steps, RW), lambda i, c: (i, c, 0))],
        out_shape=[jax.ShapeDtypeStruct((n, HD, HD), F32), jax.ShapeDtypeStruct((n // HEADS, t_len, RW), F32)],
        scratch_shapes=([pltpu.VMEM((group, HD, 2 * SCAN_L), BF16)] + [pltpu.VMEM((group, HD, 2 * LANES), F32)] * 2
                        + [pltpu.VMEM((steps, HD), F32)] * group),
        compiler_params=_cparams(("arbitrary", "arbitrary")),
        name="wkv_scan",
    )(s0, r, w, k, a, b, vt)


def _post_kernel(y_ref, g_ref, bonus_ref, pp_ref, pool0_ref, x_ref, gate_ref, lnw_ref, lnb_ref, bd_ref,
                 wpool_ref, pscale_ref, wout_ref, o_ref, ext_sc, *, tt, pos0):
    t = pl.program_id(1)

    @pl.when(t == 0)
    def _():
        ext_sc[0:16, :] = pool0_ref[0]

    @pl.when(t > 0)
    def _():
        ext_sc[0:16, :] = ext_sc[tt:tt + 16, :]

    pp = pp_ref[0]
    ext_sc[16:16 + tt, :] = pp

    bd = bd_ref[...]
    y = y_ref[0]
    cen = y - _dot3(y, bd) * (1.0 / HD)
    var = _dot3(cen * cen, bd) * (1.0 / HD)
    yn = cen * lax.rsqrt(var + LN_X_EPS)
    y_rwkv = (yn * lnw_ref[...] + lnb_ref[...] + bonus_ref[0]) * g_ref[0]

    pos = pos0 + t * tt + lax.broadcasted_iota(jnp.int32, (tt, LANES), 0)
    out = _dot(y_rwkv.astype(BF16), wout_ref[0:RW, :])
    gd = PW // len(POOL_WINDOWS)
    for gi, win in enumerate(POOL_WINDOWS):
        lanes = slice(gi * gd, (gi + 1) * gd)
        acc = ext_sc[16:16 + tt, lanes]
        for j in range(1, win):
            acc = acc + ext_sc[16 - j:16 - j + tt, lanes]
        cnt = jnp.minimum(pos + 1, win).astype(F32)
        dlt = acc / cnt - pp[:, lanes]
        yp = _dot(dlt.astype(BF16), wpool_ref[gi]) * pscale_ref[:, lanes]
        out = out + _dot(yp.astype(BF16), wout_ref[RW + gi * gd:RW + (gi + 1) * gd, :])
    o_ref[0] = x_ref[0] + gate_ref[0] * out


def rwkv_post(y, g, bonus, pp, pool0, x, gate, p, tt, pos0):
    s, t_len, _ = pp.shape
    const = lambda shape: pl.BlockSpec(shape, lambda i, j: (0,) * len(shape))
    tok = lambda wd: pl.BlockSpec((1, tt, wd), lambda i, j: (i, j, 0))
    gd = PW // len(POOL_WINDOWS)
    return pl.pallas_call(
        functools.partial(_post_kernel, tt=tt, pos0=pos0),
        grid=(s, t_len // tt),
        in_specs=[tok(RW), tok(RW), tok(RW), tok(PW),
                  pl.BlockSpec((1, 16, PW), lambda i, j: (i, 0, 0)), tok(D),
                  pl.BlockSpec((1, 1, D), lambda i, j: (i, 0, 0)),
                  const((1, RW)), const((1, RW)), const((RW, RW)), const((len(POOL_WINDOWS), gd, gd)),
                  const((1, PW)), const((D, D))],
        out_specs=tok(D),
        out_shape=jax.ShapeDtypeStruct((s, t_len, D), F32),
        scratch_shapes=[pltpu.VMEM((16 + tt, PW), F32)],
        compiler_params=_cparams(("arbitrary", "arbitrary")),
        name="rwkv_post",
    )(y, g, bonus, pp, pool0, x, gate, p["ln_w"].reshape(1, RW), p["ln_b"].reshape(1, RW), p["bd"], p["w_pool"],
      p["pool_scale"].reshape(1, PW), p["w_out"])


def _sb_block(q_h, k_bf, v_bf, u_ref, carry_ref, acc_ref, h, valid):
    tk = k_bf.shape[0]
    logit = lax.dot_general(q_h, k_bf, (((1,), (1,)), ((), ())), preferred_element_type=F32)
    sp = jnp.maximum(logit, 0.0) + jnp.log2(1.0 + jnp.exp2(-jnp.abs(logit)))
    log_1m = -sp
    if valid is not None:
        log_1m = jnp.where(valid, log_1m, 0.0)
    hi = log_1m.astype(BF16)
    lo = (log_1m - hi.astype(F32)).astype(BF16)
    u = u_ref[...]
    after = _dot(hi, u) + _dot(lo, u)
    total = jnp.broadcast_to((after + log_1m)[:, 0:1], (after.shape[0], LANES))
    carry = carry_ref[h]
    carry = carry[:, :tk] if tk <= LANES else jnp.concatenate([carry] * (tk // LANES), axis=1)
    wts = jnp.exp2(logit - sp + after + carry)
    if valid is not None:
        wts = jnp.where(valid, wts, 0.0)
    acc_ref[h] += _dot(wts.astype(BF16), v_bf)
    carry_ref[h] += total


def _sb_kernel(*refs, tq, tk, n_cache, tkc, pairs):
    if n_cache:
        q_ref, kn_ref, vn_ref, un_ref, kc_ref, vc_ref, uc_ref, o_ref, carry_ref, acc_ref = refs
    else:
        q_ref, kn_ref, vn_ref, un_ref, o_ref, carry_ref, acc_ref = refs
    qi = pl.program_id(2)
    m = tq // tk
    lane = lax.broadcasted_iota(jnp.int32, (tq, LANES), 1)
    q_pairs = []
    for p in range(pairs):
        q2 = q_ref[0, :, p * LANES:(p + 1) * LANES] * (HD ** -0.5 * LOG2_E)
        q_pairs.append(jnp.concatenate([jnp.where(lane < HD, q2, 0.0), jnp.where(lane >= HD, q2, 0.0)],
                                       axis=0).astype(BF16))
    carry_ref[...] = jnp.zeros_like(carry_ref)
    acc_ref[...] = jnp.zeros_like(acc_ref)

    def visit(k_ref, v_ref, start, rows, u_ref, valid):
        if valid is not None:
            valid = jnp.concatenate([valid, valid], axis=0)
        for p in range(pairs):
            kb = k_ref[0, pl.ds(start, rows), p * LANES:(p + 1) * LANES].astype(BF16)
            vb = v_ref[0, pl.ds(start, rows), p * LANES:(p + 1) * LANES].astype(BF16)
            _sb_block(q_pairs[p], kb, vb, u_ref, carry_ref, acc_ref, p, valid)

    qpos = qi * tq + lax.broadcasted_iota(jnp.int32, (tq, tk), 0)
    col = lax.broadcasted_iota(jnp.int32, (tq, tk), 1)
    for i in range(m):
        start = pl.multiple_of((qi * m + (m - 1 - i)) * tk, tk)
        visit(kn_ref, vn_ref, start, tk, un_ref, start + col < qpos)

    def earlier(it, c):
        visit(kn_ref, vn_ref, pl.multiple_of((qi * m - 1 - it) * tk, tk), tk, un_ref, None)
        return c

    lax.fori_loop(0, qi * m, earlier, 0)

    if n_cache:
        def cached(it, c):
            visit(kc_ref, vc_ref, pl.multiple_of((n_cache - 1 - it) * tkc, tkc), tkc, uc_ref, None)
            return c

        lax.fori_loop(0, n_cache, cached, 0)

    for p in range(pairs):
        o_ref[0, :, p * LANES:(p + 1) * LANES] = jnp.where(lane < HD, acc_ref[p, 0:tq], acc_ref[p, tq:2 * tq])


def _cumsum_matrix(tk):
    return jnp.asarray(np.tril(np.ones((tk, tk)), -1), BF16)


def stick_breaking(q, k, v, ck, cv, tq, tk, tkc, pairs):
    b, t_len, _ = q.shape
    n_cache = 0 if ck is None else ck.shape[1] // tkc
    wd = pairs * LANES
    cols = lambda rows: pl.BlockSpec((1, rows, wd), lambda i, hp, qi: (i, 0, hp))
    const = lambda shape: pl.BlockSpec(shape, lambda i, hp, qi: (0, 0))
    qspec = pl.BlockSpec((1, tq, wd), lambda i, hp, qi: (i, qi, hp))
    in_specs = [qspec, cols(t_len), cols(t_len), const((tk, tk))]
    args = [q, k, v, _cumsum_matrix(tk)]
    if n_cache:
        in_specs += [cols(ck.shape[1]), cols(ck.shape[1]), const((tkc, tkc))]
        args += [ck, cv, _cumsum_matrix(tkc)]
    return pl.pallas_call(
        functools.partial(_sb_kernel, tq=tq, tk=tk, n_cache=n_cache, tkc=tkc, pairs=pairs),
        grid=(b, D // wd, t_len // tq),
        in_specs=in_specs,
        out_specs=qspec,
        out_shape=jax.ShapeDtypeStruct((b, t_len, D), F32),
        scratch_shapes=[pltpu.VMEM((pairs, 2 * tq, LANES), F32), pltpu.VMEM((pairs, 2 * tq, LANES), F32)],
        compiler_params=_cparams(("arbitrary", "arbitrary", "arbitrary")),
        name="stick_breaking",
    )(*args)


def _proj_res_kernel(a_ref, w_ref, x_ref, gate_ref, o_ref):
    y = _dot(a_ref[...].astype(BF16), w_ref[...])
    for q in range(TM // CHUNK):
        rows = slice(q * CHUNK, (q + 1) * CHUNK)
        o_ref[rows, :] = x_ref[rows, :] + gate_ref[0, q:q + 1, :] * y[rows, :]


def proj_residual(a, w_bf, x, gate):
    nt = x.shape[0]
    q = TM // CHUNK
    return pl.pallas_call(
        _proj_res_kernel,
        grid=(nt // TM,),
        in_specs=[pl.BlockSpec((TM, D), lambda i: (i, 0)), pl.BlockSpec((D, D), lambda i: (0, 0)),
                  pl.BlockSpec((TM, D), lambda i: (i, 0)), pl.BlockSpec((1, q, D), lambda i: (i, 0, 0))],
        out_specs=pl.BlockSpec((TM, D), lambda i: (i, 0)),
        out_shape=jax.ShapeDtypeStruct((nt, D), F32),
        compiler_params=_cparams(("arbitrary",)),
        name="proj_residual",
    )(a, w_bf, x, gate)


def _route_kernel(x_ref, g_ref, sh_ref, sc_ref, wr_ref, br_ref, tri_ref, h_o, mi_o, mf_o, cnt_o, carry_sc):
    i = pl.program_id(0)

    @pl.when(i == 0)
    def _():
        carry_sc[...] = jnp.zeros_like(carry_sc)

    _mod_tile(x_ref, g_ref, sh_ref, sc_ref, h_o)
    h = h_o[...]
    h_hi = h.astype(BF16)
    h_lo = (h - h_hi.astype(F32)).astype(BF16)
    w_hi = wr_ref[0]
    w_lo = wr_ref[1]
    lg = _dot(h_hi, w_hi) + _dot(h_hi, w_lo) + _dot(h_lo, w_hi) + br_ref[...]
    lane = lax.broadcasted_iota(jnp.int32, lg.shape, 1)
    lane_f = lane.astype(F32)
    neg = jnp.float32(-3.0e38)
    vals, idxs = [], []
    mask = jnp.zeros(lg.shape, F32)
    for _ in range(TOP_K):
        m = jnp.max(lg, axis=1, keepdims=True)
        idx = jnp.min(jnp.where(lg == m, lane_f, float(LANES)), axis=1, keepdims=True)
        sel = lane_f == idx
        mask = jnp.where(sel, 1.0, mask)
        lg = jnp.where(sel, neg, lg)
        vals.append(m)
        idxs.append(idx)
    es = [jnp.exp(vj - vals[0]) for vj in vals]
    den = es[0] + es[1] + es[2] + es[3]
    rank = _dot(tri_ref[...], mask.astype(BF16)) + carry_sc[...]
    carry_sc[...] += jnp.sum(mask, axis=0, keepdims=True)
    mi = jnp.zeros(lg.shape, F32)
    mf = jnp.zeros(lg.shape, F32)
    for j in range(TOP_K):
        rj = jnp.sum(jnp.where(lane_f == idxs[j], rank, 0.0), axis=1, keepdims=True)
        mi = jnp.where(lane == j, idxs[j], mi)
        mi = jnp.where(lane == TOP_K + j, rj, mi)
        mf = jnp.where(lane == j, es[j] / den, mf)
    mi_o[...] = mi.astype(jnp.int32)
    mf_o[...] = mf
    cnt_o[...] = jnp.broadcast_to(carry_sc[...], cnt_o.shape)


def moe_route(x, g, sh, sc, wr2, br, tri):
    nt = x.shape[0]
    q = TM // CHUNK
    return pl.pallas_call(
        _route_kernel,
        grid=(nt // TM,),
        in_specs=[pl.BlockSpec((TM, D), lambda i: (i, 0)), pl.BlockSpec((1, D), lambda i: (0, 0)),
                  pl.BlockSpec((1, q, D), lambda i: (i, 0, 0)), pl.BlockSpec((1, q, D), lambda i: (i, 0, 0)),
                  pl.BlockSpec((2, D, LANES), lambda i: (0, 0, 0)), pl.BlockSpec((1, LANES), lambda i: (0, 0)),
                  pl.BlockSpec((TM, TM), lambda i: (0, 0))],
        out_specs=[pl.BlockSpec((TM, D), lambda i: (i, 0)), pl.BlockSpec((TM, LANES), lambda i: (i, 0)),
                   pl.BlockSpec((TM, LANES), lambda i: (i, 0)), pl.BlockSpec((8, LANES), lambda i: (0, 0))],
        out_shape=[jax.ShapeDtypeStruct((nt, D), F32), jax.ShapeDtypeStruct((nt, LANES), jnp.int32),
                   jax.ShapeDtypeStruct((nt, LANES), F32), jax.ShapeDtypeStruct((8, LANES), F32)],
        scratch_shapes=[pltpu.VMEM((1, LANES), F32)],
        compiler_params=_cparams(("arbitrary",)),
        name="moe_route",
    )(x, g, sh, sc, wr2, br, tri)


def _dispatch_kernel(dest_ref, h_ref, xs_in, xs_out, sem):
    del xs_in

    def copy(r, j):
        return pltpu.make_async_copy(h_ref.at[pl.ds(r, 1)], xs_out.at[pl.ds(dest_ref[r * TOP_K + j], 1)], sem)

    def issue(r, c):
        for j in range(TOP_K):
            copy(r, j).start()
        return c

    def drain(r, c):
        for j in range(TOP_K):
            copy(r, j).wait()
        return c

    lax.fori_loop(0, TM, issue, 0)
    lax.fori_loop(0, TM, drain, 0)


def moe_dispatch(dest_flat, h, xs_zero):
    nt = h.shape[0]
    return pl.pallas_call(
        _dispatch_kernel,
        grid=(nt // TM,),
        in_specs=[pl.BlockSpec((TM * TOP_K,), lambda i: (i,), memory_space=pltpu.SMEM),
                  pl.BlockSpec((TM, D), lambda i: (i, 0)),
                  pl.BlockSpec(memory_space=pl.ANY)],
        out_specs=pl.BlockSpec(memory_space=pl.ANY),
        out_shape=jax.ShapeDtypeStruct(xs_zero.shape, xs_zero.dtype),
        scratch_shapes=[pltpu.SemaphoreType.DMA(())],
        input_output_aliases={2: 0},
        compiler_params=_cparams(("arbitrary",)),
        name="moe_dispatch",
    )(dest_flat, h, xs_zero)


def _expert_kernel(te_ref, nu_ref, x_ref, wu_ref, bu_ref, wd_ref, bd_ref, o_ref):
    i = pl.program_id(0)

    @pl.when(i < nu_ref[0])
    def _():
        x = x_ref[...].astype(BF16)
        u = _dot(x, wu_ref[0]) + bu_ref[0]
        nxt = pltpu.roll(u, u.shape[1] - 1, axis=1)
        glu = jnp.minimum(u, SWIGLU_LIMIT)
        lin = jnp.clip(nxt, -SWIGLU_LIMIT, SWIGLU_LIMIT)
        act = glu * jax.nn.sigmoid(SWIGLU_ALPHA * glu) * (lin + 1.0)
        lane = lax.broadcasted_iota(jnp.int32, act.shape, 1)
        act = jnp.where(lane % 2 == 0, act, 0.0)
        o_ref[...] = _dot(act.astype(BF16), wd_ref[0]) + bd_ref[0]

    @pl.when(i >= nu_ref[0])
    def _():
        o_ref[...] = jnp.zeros_like(o_ref)


def moe_experts(tile_expert, n_used, xs, wu, bu, wd, bd):
    rows = xs.shape[0]
    xspec = pl.BlockSpec((TME, D), lambda i, te, nu: (i, 0))
    return pl.pallas_call(
        _expert_kernel,
        grid_spec=pltpu.PrefetchScalarGridSpec(
            num_scalar_prefetch=2, grid=(rows // TME,),
            in_specs=[xspec,
                      pl.BlockSpec((1, D, 2 * D), lambda i, te, nu: (te[i], 0, 0)),
                      pl.BlockSpec((1, 1, 2 * D), lambda i, te, nu: (te[i], 0, 0)),
                      pl.BlockSpec((1, 2 * D, D), lambda i, te, nu: (te[i], 0, 0)),
                      pl.BlockSpec((1, 1, D), lambda i, te, nu: (te[i], 0, 0))],
            out_specs=xspec),
        out_shape=jax.ShapeDtypeStruct((rows, D), F32),
        compiler_params=_cparams(("arbitrary",)),
        name="moe_experts",
    )(tile_expert, n_used, xs, wu, bu, wd, bd)


def _combine_kernel(dest_ref, x_ref, mf_ref, gate_ref, ys_ref, o_ref, buf, sem):
    def copy(r, j):
        return pltpu.make_async_copy(ys_ref.at[pl.ds(dest_ref[r * TOP_K + j], 1)], buf.at[j, pl.ds(r, 1)], sem)

    def issue(r, c):
        for j in range(TOP_K):
            copy(r, j).start()
        return c

    def drain(r, c):
        for j in range(TOP_K):
            copy(r, j).wait()
        return c

    lax.fori_loop(0, TM, issue, 0)
    lax.fori_loop(0, TM, drain, 0)
    mf = mf_ref[...]
    y = buf[0] * mf[:, 0:1]
    for j in range(1, TOP_K):
        y = y + buf[j] * mf[:, j:j + 1]
    for q in range(TM // CHUNK):
        rows = slice(q * CHUNK, (q + 1) * CHUNK)
        o_ref[rows, :] = x_ref[rows, :] + gate_ref[0, q:q + 1, :] * y[rows, :]


def moe_combine(dest_flat, x, mf, gate, ys):
    nt = x.shape[0]
    q = TM // CHUNK
    return pl.pallas_call(
        _combine_kernel,
        grid=(nt // TM,),
        in_specs=[pl.BlockSpec((TM * TOP_K,), lambda i: (i,), memory_space=pltpu.SMEM),
                  pl.BlockSpec((TM, D), lambda i: (i, 0)), pl.BlockSpec((TM, LANES), lambda i: (i, 0)),
                  pl.BlockSpec((1, q, D), lambda i: (i, 0, 0)), pl.BlockSpec(memory_space=pl.ANY)],
        out_specs=pl.BlockSpec((TM, D), lambda i: (i, 0)),
        out_shape=jax.ShapeDtypeStruct((nt, D), F32),
        scratch_shapes=[pltpu.VMEM((TOP_K, TM, D), F32), pltpu.SemaphoreType.DMA(())],
        compiler_params=_cparams(("arbitrary",)),
        name="moe_combine",
    )(dest_flat, x, mf, gate, ys)


def moe_layer(x, g, sh, sc, gate, mp, xs_init=None):
    nt = x.shape[0]
    h, mi, mf, cnt = moe_route(x, g, sh, sc, mp["wr2"], mp["br"], mp["tri"])
    counts = cnt[0, :N_EXPERTS].astype(jnp.int32)
    padded = (counts + TME - 1) // TME * TME
    pad_end = jnp.cumsum(padded)
    pad_start = pad_end - padded
    dest = pad_start[mi[:, :TOP_K]] + mi[:, TOP_K:2 * TOP_K]
    dest_flat = dest.reshape(-1).astype(jnp.int32)
    n_tiles = (nt * TOP_K + N_EXPERTS * (TME - 1)) // TME
    tile_start = jnp.arange(n_tiles, dtype=jnp.int32) * TME
    tile_expert = jnp.minimum(jnp.sum(pad_end[None, :] <= tile_start[:, None], axis=1), N_EXPERTS - 1).astype(jnp.int32)
    n_used = (pad_end[-1:] // TME).astype(jnp.int32)
    if xs_init is None:
        xs_init = jnp.zeros((n_tiles * TME, D), F32)
    xs = moe_dispatch(dest_flat, h, xs_init)
    ys = moe_experts(tile_expert, n_used, xs, mp["wu"], mp["bu"], mp["wd"], mp["bd"])
    return moe_combine(dest_flat, x, mf, gate, ys), ys


def _final_kernel(x_ref, g_ref, o_ref):
    x = x_ref[...]
    ms = jnp.mean(x * x, axis=-1, keepdims=True)
    o_ref[...] = x * lax.rsqrt(ms + EPS) * g_ref[...]


def final_norm(x, g):
    nt = x.shape[0]
    return pl.pallas_call(
        _final_kernel,
        grid=(nt // TM,),
        in_specs=[pl.BlockSpec((TM, D), lambda i: (i, 0)), pl.BlockSpec((1, D), lambda i: (0, 0))],
        out_specs=pl.BlockSpec((TM, D), lambda i: (i, 0)),
        out_shape=jax.ShapeDtypeStruct((nt, D), F32),
        compiler_params=_cparams(("arbitrary",)),
        name="final_norm",
    )(x, g.reshape(1, D))


def rwkv_pool_layer(x, mods, g, lp, groups, state):
    sh, sc, gate_rows = mods
    ps, pp = normmm(x, g, sh, sc, lp["w_in"], (SHIFT_W, PW))
    outs, states = [], []
    for (r0, s, t_len, tt, pos0, group), (wkv0, shift0, pool0) in zip(groups, state):
        n = s * t_len
        ps_g = ps[r0:r0 + n].reshape(s, t_len, SHIFT_W)
        pp_g = pp[r0:r0 + n].reshape(s, t_len, PW)
        r, w, k, a, b, vt, gt, bonus = rwkv_prep(ps_g, shift0, lp, tt)
        flat = lambda z: z.reshape(s * HEADS, t_len, HD)
        s_new, y = wkv_scan(wkv0.reshape(s * HEADS, HD, HD), flat(r), flat(w), flat(k), flat(a), flat(b),
                            vt.reshape(s * HEADS, HD, vt.shape[2]), t_len, group)
        pool_pad = jnp.concatenate([jnp.zeros((s, 1, PW), F32), pool0], axis=1)
        x_g = x[r0:r0 + n].reshape(s, t_len, D)
        out = rwkv_post(y, gt, bonus, pp_g, pool_pad, x_g,
                        gate_rows[r0 // CHUNK:(r0 + n) // CHUNK:t_len // CHUNK].reshape(s, 1, D), lp, tt, pos0)
        outs.append(out.reshape(n, D))
        states.append((s_new.reshape(s, HEADS, HD, HD), ps_g[:, -1], pp_g[:, t_len - POOL_HIST:]))
    return jnp.concatenate(outs, axis=0), states


def attention_layer(x, mods, g, lp, groups, caches):
    sh, sc, gate = mods
    q, k, v = normmm(x, g, sh, sc, lp["w_qkv"], (D, D, D))
    outs, kvs = [], []
    for (r0, s, t_len, tq, tk, tkc, pairs), cache in zip(groups, caches):
        n = s * t_len
        seq = lambda z: z[r0:r0 + n].reshape(s, t_len, D)
        ck, cv = cache
        outs.append(stick_breaking(seq(q), seq(k), seq(v), ck, cv, tq, tk, tkc, pairs).reshape(n, D))
        kvs.append((seq(k).reshape(s, t_len, D // HD, HD), seq(v).reshape(s, t_len, D // HD, HD)))
    o = jnp.concatenate(outs, axis=0)
    return proj_residual(o, lp["w_out"], x, gate), kvs


def kernel(x_prompt, x_sample, c_prompt, c_sample, state_wkv, state_shift, state_pool, cache_k, cache_v, ada_w, ada_b, norm_mix_g, norm_ffn_g, final_g, a_w_in, a_mu, a_w0, a_w_decay, a_a0, a_w_iclr, a_w_gate, a_k_k, a_k_a, a_r_k, a_ln_w, a_ln_b, b_w_pool, b_pool_scale, ab_w_out, c_w_qkv, c_w_out, moe_w_router, moe_b_router, moe_w_up, moe_b_up, moe_w_down, moe_b_down):
    bp, tp, _ = x_prompt.shape
    bs, ts, _ = x_sample.shape
    past = cache_k.shape[2]
    n_p, n_s = bp * tp, bs * ts
    nt = n_p + n_s
    x = jnp.concatenate([x_prompt.reshape(n_p, D), x_sample.reshape(n_s, D)], axis=0)

    mod = ada_mod(jnp.concatenate([c_prompt, c_sample], axis=0), ada_w, ada_b)
    seq_of_chunk = np.concatenate([np.repeat(np.arange(bp), tp // CHUNK), bp + np.repeat(np.arange(bs), ts // CHUNK)])
    q = TM // CHUNK

    def mod_rows(layer, idx):
        rows = mod[layer, :, idx * D:(idx + 1) * D][seq_of_chunk]
        return rows, rows.reshape(nt // TM, q, D)

    zeros = lambda *shape: jnp.zeros(shape, F32)
    bd = jnp.asarray(np.kron(np.eye(HEADS), np.ones((HD, HD))), BF16)
    tri = jnp.asarray(np.tril(np.ones((TM, TM)), -1), BF16)

    def moe_params(layer):
        wr = jnp.pad(moe_w_router[layer], ((0, 0), (0, LANES - N_EXPERTS)))
        wr_hi = wr.astype(BF16)
        wr_lo = (wr - wr_hi.astype(F32)).astype(BF16)
        br = jnp.concatenate([moe_b_router[layer], jnp.full((LANES - N_EXPERTS,), -1e30, F32)]).reshape(1, LANES)
        wd = moe_w_down[layer].astype(BF16)
        wd = jnp.stack([wd, jnp.zeros_like(wd)], axis=2).reshape(N_EXPERTS, 2 * D, D)
        return dict(wr2=jnp.stack([wr_hi, wr_lo]), br=br, tri=tri,
                    wu=moe_w_up[layer].astype(BF16), bu=moe_b_up[layer][:, None, :],
                    wd=wd, bd=moe_b_down[layer][:, None, :])

    w_lora = jnp.zeros((128, 2 * RW), F32).at[:64, :RW].set(a_w_decay[0]).at[64:, RW:].set(a_w_iclr[0])
    lp0 = dict(w_in=a_w_in[0].astype(BF16), mu=a_mu[0], w0=a_w0[0], w_lora=w_lora.astype(BF16), a0=a_a0[0],
               w_gate=a_w_gate[0].astype(BF16), k_k=a_k_k[0], k_a=a_k_a[0], r_k=a_r_k[0], bd=bd,
               ln_w=a_ln_w[0], ln_b=a_ln_b[0], w_pool=b_w_pool[0].astype(BF16), pool_scale=b_pool_scale[0],
               w_out=ab_w_out[0].astype(BF16))
    lp1 = dict(w_qkv=c_w_qkv[0].astype(BF16), w_out=c_w_out[0].astype(BF16))

    sh1, sc1, g1 = (mod_rows(0, i) for i in (0, 1, 2))
    groups0 = [(0, bp, tp, 256, 0, 16), (n_p, bs, ts, ts, past, 16)]
    state0 = [(zeros(bp, HEADS, HD, HD), zeros(bp, SHIFT_W), zeros(bp, POOL_HIST, PW)),
              (state_wkv[0], state_shift[0], state_pool[0])]
    x, st0 = rwkv_pool_layer(x, (sh1[1], sc1[1], g1[0]), norm_mix_g[0].reshape(1, D), lp0, groups0, state0)
    sh2, sc2, g2 = (mod_rows(0, i) for i in (3, 4, 5))
    x, ys0 = moe_layer(x, norm_ffn_g[0].reshape(1, D), sh2[1], sc2[1], g2[1], moe_params(0))

    sh1, sc1, g1 = (mod_rows(1, i) for i in (0, 1, 2))
    groups1 = [(0, bp, tp, 512, 256, 256, 1), (n_p, bs, ts, ts, ts, 512, 4)]
    caches = [(None, None), (cache_k[0].reshape(bs, past, D), cache_v[0].reshape(bs, past, D))]
    x, kv1 = attention_layer(x, (sh1[1], sc1[1], g1[1]), norm_mix_g[1].reshape(1, D), lp1, groups1, caches)
    sh2, sc2, g2 = (mod_rows(1, i) for i in (3, 4, 5))
    x, _ = moe_layer(x, norm_ffn_g[1].reshape(1, D), sh2[1], sc2[1], g2[1], moe_params(1), xs_init=ys0)

    y = final_norm(x, final_g)
    lead = lambda z: z[None]
    (p_wkv, p_shift, p_pool), (s_wkv, s_shift, s_pool) = st0
    (p_k, p_v), (s_k, s_v) = kv1
    return (y[:n_p].reshape(bp, tp, D), y[n_p:].reshape(bs, ts, D),
            lead(p_wkv), lead(p_shift), lead(p_pool), lead(p_k), lead(p_v),
            lead(s_wkv), lead(s_shift), lead(s_pool), lead(s_k), lead(s_v))
```

```python
import functools

import jax
import jax.numpy as jnp
import numpy as np
from jax import lax
from jax.experimental import pallas as pl
from jax.experimental.pallas import tpu as pltpu

F32 = jnp.float32
BF16 = jnp.bfloat16

D = 1024
CHUNK = 64
HEADS = 8
HD = 64
RW = HEADS * HD
PW = D - RW
POOL_WINDOWS = (2, 4, 8, 16)
POOL_HIST = 15
SHIFT_W = 3 * RW + 64 + 64 + 128
N_EXPERTS = 32
TOP_K = 4
EPS = 1e-6
LN_X_EPS = 64e-5
SWIGLU_LIMIT = 7.0
SWIGLU_ALPHA = 1.702
LOG2_E = 1.4426950408889634
CARRY_FLOOR = -200.0

LANES = 128
TM = 256
TME = 256
SCAN_L = 128
VMEM_LIMIT = 56 * 1024 * 1024


def _cparams(sem):
    return pltpu.CompilerParams(dimension_semantics=sem, vmem_limit_bytes=VMEM_LIMIT)


def _softplus(z):
    return jnp.maximum(z, 0.0) + jnp.log(1.0 + jnp.exp(-jnp.abs(z)))


def _split3(x):
    hi = x.astype(BF16)
    r1 = x - hi.astype(F32)
    mid = r1.astype(BF16)
    lo = (r1 - mid.astype(F32)).astype(BF16)
    return hi, mid, lo


def _dot(a, b):
    return jnp.dot(a, b, preferred_element_type=F32)


def _dot3(x, w01):
    hi, mid, lo = _split3(x)
    return _dot(hi, w01) + _dot(mid, w01) + _dot(lo, w01)


def _mod_tile(x_ref, g_ref, sh_ref, sc_ref, h_ref):
    g = g_ref[...]
    for q in range(x_ref.shape[0] // CHUNK):
        x = x_ref[q * CHUNK:(q + 1) * CHUNK, :]
        ms = jnp.mean(x * x, axis=-1, keepdims=True)
        y = x * lax.rsqrt(ms + EPS) * g
        h = y * (1.0 + sc_ref[0, q:q + 1, :]) + sh_ref[0, q:q + 1, :]
        h_ref[q * CHUNK:(q + 1) * CHUNK, :] = h.astype(h_ref.dtype)


def _ada_kernel(c_ref, w_ref, b_ref, o_ref):
    c = c_ref[...]
    s = c * jax.nn.sigmoid(c)
    o_ref[0] = _dot(s.astype(BF16), w_ref[0].astype(BF16)) + b_ref[0]


def ada_mod(c_all, ada_w, ada_b):
    depth, _, n = ada_w.shape
    s = c_all.shape[0]
    tn = 1536
    return pl.pallas_call(
        _ada_kernel,
        grid=(depth, n // tn),
        in_specs=[pl.BlockSpec((s, D), lambda l, j: (0, 0)),
                  pl.BlockSpec((1, D, tn), lambda l, j: (l, 0, j)),
                  pl.BlockSpec((1, 1, tn), lambda l, j: (l, 0, j))],
        out_specs=pl.BlockSpec((1, s, tn), lambda l, j: (l, 0, j)),
        out_shape=jax.ShapeDtypeStruct((depth, s, n), F32),
        compiler_params=_cparams(("arbitrary", "arbitrary")),
        name="ada_mod",
    )(c_all, ada_w, ada_b.reshape(depth, 1, n))


def _normmm_kernel(x_ref, g_ref, sh_ref, sc_ref, w_ref, *rest, widths):
    outs, h_ref = rest[:-1], rest[-1]
    _mod_tile(x_ref, g_ref, sh_ref, sc_ref, h_ref)
    h = h_ref[...]
    off = 0
    for o_ref, wd in zip(outs, widths):
        o_ref[...] = _dot(h, w_ref[:, off:off + wd])
        off += wd


def normmm(x, g, sh, sc, w_bf, widths):
    nt = x.shape[0]
    n = w_bf.shape[1]
    q = TM // CHUNK
    return pl.pallas_call(
        functools.partial(_normmm_kernel, widths=widths),
        grid=(nt // TM,),
        in_specs=[pl.BlockSpec((TM, D), lambda i: (i, 0)),
                  pl.BlockSpec((1, D), lambda i: (0, 0)),
                  pl.BlockSpec((1, q, D), lambda i: (i, 0, 0)),
                  pl.BlockSpec((1, q, D), lambda i: (i, 0, 0)),
                  pl.BlockSpec((D, n), lambda i: (0, 0))],
        out_specs=[pl.BlockSpec((TM, wd), lambda i: (i, 0)) for wd in widths],
        out_shape=[jax.ShapeDtypeStruct((nt, wd), F32) for wd in widths],
        scratch_shapes=[pltpu.VMEM((TM, D), BF16)],
        compiler_params=_cparams(("arbitrary",)),
        name="normmm",
    )(x, g, sh, sc, w_bf)


def _t_pad(x, rows):
    m = x.shape[0]
    if m < rows:
        x = jnp.concatenate([x, jnp.zeros((rows - m, x.shape[1]), x.dtype)], axis=0)
    return x.T


def _prep_kernel(ps_ref, s0_ref, mu_ref, w0_ref, wlora_ref, a0_ref, wgate_ref, kk_ref, ka_ref, rk_ref,
                 bd_ref, r_o, w_o, k_o, a_o, b_o, vt_o, g_o, bonus_o, prev_sc, *, tt, ttp):
    t = pl.program_id(1)

    @pl.when(t == 0)
    def _():
        prev_sc[...] = s0_ref[0]

    ps = ps_ref[0]
    rolled = pltpu.roll(ps, 1, axis=0)
    row = lax.broadcasted_iota(jnp.int32, ps.shape, 0)
    prev = jnp.where(row == 0, prev_sc[...], rolled)
    prev_sc[...] = ps[tt - 1:tt, :]
    xs = ps + (prev - ps) * mu_ref[...]
    r = xs[:, 0:RW]
    k = xs[:, RW:2 * RW]
    v = xs[:, 2 * RW:3 * RW]
    xwa = xs[:, 3 * RW:3 * RW + 128]
    xg = xs[:, 3 * RW + 128:3 * RW + 256]
    lane = lax.broadcasted_iota(jnp.int32, xwa.shape, 1)
    lin = jnp.where(lane < 64, jnp.tanh(xwa), xwa)
    lora = _dot(lin.astype(BF16), wlora_ref[...])
    log_w = -_softplus(-(w0_ref[...] + lora[:, :RW])) - 0.5
    decay = jnp.exp(-jnp.exp(log_w))
    a = jax.nn.sigmoid(a0_ref[...] + lora[:, RW:])
    g_o[0] = _dot(jax.nn.sigmoid(xg).astype(BF16), wgate_ref[...])
    bd = bd_ref[...]
    kk = k * kk_ref[...]
    n2 = _dot3(kk * kk, bd)
    kk = kk * lax.rsqrt(jnp.maximum(n2, 1e-24))
    k_mod = k * (1.0 + (a - 1.0) * ka_ref[...])
    bonus_o[0] = _dot3(r * k_mod * rk_ref[...], bd) * v
    a_vec = -kk
    b_vec = kk * a
    for h in range(HEADS):
        sl = slice(h * HD, (h + 1) * HD)
        r_o[0, h] = r[:, sl]
        w_o[0, h] = decay[:, sl]
        k_o[0, h] = k_mod[:, sl]
        a_o[0, h] = a_vec[:, sl]
        b_o[0, h] = b_vec[:, sl]
    vt_o[0] = _t_pad(v, ttp)


def rwkv_prep(ps, shift0, p, tt):
    s, t_len, _ = ps.shape
    ttp = max(tt, LANES)
    tpad = (t_len // tt) * ttp
    row = lambda a: a.reshape(1, -1)
    const = lambda shape: pl.BlockSpec(shape, lambda i, j: (0,) * len(shape))
    head_spec = pl.BlockSpec((1, HEADS, tt, HD), lambda i, j: (i, 0, j, 0))
    head_shape = jax.ShapeDtypeStruct((s, HEADS, t_len, HD), F32)
    tok_spec = pl.BlockSpec((1, tt, RW), lambda i, j: (i, j, 0))
    tok_shape = jax.ShapeDtypeStruct((s, t_len, RW), F32)
    return pl.pallas_call(
        functools.partial(_prep_kernel, tt=tt, ttp=ttp),
        grid=(s, t_len // tt),
        in_specs=[pl.BlockSpec((1, tt, SHIFT_W), lambda i, j: (i, j, 0)),
                  pl.BlockSpec((1, 1, SHIFT_W), lambda i, j: (i, 0, 0)),
                  const((1, SHIFT_W)), const((1, RW)), const((128, 2 * RW)), const((1, RW)),
                  const((128, RW)), const((1, RW)), const((1, RW)), const((1, RW)), const((RW, RW))],
        out_specs=[head_spec] * 5 + [pl.BlockSpec((1, RW, ttp), lambda i, j: (i, 0, j)), tok_spec, tok_spec],
        out_shape=[head_shape] * 5 + [jax.ShapeDtypeStruct((s, RW, tpad), F32), tok_shape, tok_shape],
        scratch_shapes=[pltpu.VMEM((1, SHIFT_W), F32)],
        compiler_params=_cparams(("arbitrary", "arbitrary")),
        name="rwkv_prep",
    )(ps, shift0.reshape(s, 1, SHIFT_W), row(p["mu"]), row(p["w0"]), p["w_lora"], row(p["a0"]),
      p["w_gate"], row(p["k_k"]), row(p["k_a"]), row(p["r_k"]), p["bd"])


def _scan_kernel(s0_ref, r_ref, w_ref, k_ref, a_ref, b_ref, vt_ref, s_ref, y_ref, vsplit_sc, vbc_a, vbc_b, *y_scs,
                 steps, group):
    c = pl.program_id(1)

    @pl.when(c == 0)
    def _():
        s_ref[...] = s0_ref[...]

    for g in range(group):
        vt = vt_ref[g]
        hi = vt.astype(BF16)
        vsplit_sc[g, :, 0:SCAN_L] = hi
        vsplit_sc[g, :, SCAN_L:2 * SCAN_L] = (vt - hi.astype(F32)).astype(BF16)
        y_scs[g][...] = jnp.zeros_like(y_scs[g])
    sel_row = jnp.bitwise_and(lax.broadcasted_iota(jnp.int32, (2 * SCAN_L, 2 * LANES), 0), SCAN_L - 1)
    sel_blk = lax.broadcasted_iota(jnp.int32, (2 * SCAN_L, 2 * LANES), 1) // LANES
    row16 = lax.broadcasted_iota(jnp.int32, (16, HD), 0)
    last = (((1,), (1,)), ((), ()))

    def select_columns(t0, vbc):
        sel = jnp.where(sel_row == t0 + sel_blk, 1.0, 0.0).astype(BF16)
        for g in range(group):
            vbc[g] = _dot(vsplit_sc[g], sel)

    def one_step(t, vbc, u):
        sub = jnp.bitwise_and(t, 7)
        base = pl.multiple_of(t - sub, 8)
        for g in range(group):
            row = lambda ref: ref[g, pl.ds(t, 1), :]
            st = s_ref[g]
            sa = jnp.sum(st * row(a_ref), axis=1, keepdims=True)
            st = st * row(w_ref) + sa * row(b_ref) + vbc[g][:, u * LANES:u * LANES + HD] * row(k_ref)
            s_ref[g] = st
            r = row(r_ref)
            r_hi = r.astype(BF16).astype(F32)
            lhs = jnp.where(row16 == sub, r_hi, jnp.where(row16 == 8 + sub, r - r_hi, 0.0)).astype(BF16)
            yy = lax.dot_general(lhs, st.astype(BF16), last, preferred_element_type=F32)
            y_scs[g][pl.ds(base, 8), :] += yy[0:8] + yy[8:16]

    def four_steps(i, carry):
        t0 = 4 * i
        select_columns(t0 + 2, vbc_b)
        one_step(t0, vbc_a, 0)
        one_step(t0 + 1, vbc_a, 1)
        select_columns(t0 + 4, vbc_a)
        one_step(t0 + 2, vbc_b, 0)
        one_step(t0 + 3, vbc_b, 1)
        return carry

    select_columns(0, vbc_a)
    lax.fori_loop(0, steps // 4, four_steps, 0)
    for g in range(group):
        y_ref[g // HEADS, :, (g % HEADS) * HD:(g % HEADS + 1) * HD] = y_scs[g][...]


def wkv_scan(s0, r, w, k, a, b, vt, t_len, group):
    n = s0.shape[0]
    steps = min(t_len, SCAN_L)
    vec = pl.BlockSpec((group, steps, HD), lambda i, c: (i, c, 0))
    st = pl.BlockSpec((group, HD, HD), lambda i, c: (i, 0, 0))
    tr = pl.BlockSpec((group, HD, SCAN_L), lambda i, c: (i, 0, c))
    return pl.pallas_call(
        functools.partial(_scan_kernel, steps=steps, group=group),
        grid=(n // group, t_len // steps),
        in_specs=[st, vec, vec, vec, vec, vec, tr],
        out_specs=[st, pl.BlockSpec((group // HEADS, steps, RW), lambda i, c: (i, c, 0))],
        out_shape=[jax.ShapeDtypeStruct((n, HD, HD), F32), jax.ShapeDtypeStruct((n // HEADS, t_len, RW), F32)],
        scratch_shapes=([pltpu.VMEM((group, HD, 2 * SCAN_L), BF16)] + [pltpu.VMEM((group, HD, 2 * LANES), F32)] * 2
                        + [pltpu.VMEM((steps, HD), F32)] * group),
        compiler_params=_cparams(("arbitrary", "arbitrary")),
        name="wkv_scan",
    )(s0, r, w, k, a, b, vt)


def _post_kernel(y_ref, g_ref, bonus_ref, pp_ref, pool0_ref, x_ref, gate_ref, lnw_ref, lnb_ref, bd_ref,
                 wpool_ref, pscale_ref, wout_ref, o_ref, ext_sc, *, tt, pos0):
    t = pl.program_id(1)

    @pl.when(t == 0)
    def _():
        ext_sc[0:16, :] = pool0_ref[0]

    @pl.when(t > 0)
    def _():
        ext_sc[0:16, :] = ext_sc[tt:tt + 16, :]

    pp = pp_ref[0]
    ext_sc[16:16 + tt, :] = pp

    bd = bd_ref[...]
    y = y_ref[0]
    cen = y - _dot3(y, bd) * (1.0 / HD)
    var = _dot3(cen * cen, bd) * (1.0 / HD)
    yn = cen * lax.rsqrt(var + LN_X_EPS)
    y_rwkv = (yn * lnw_ref[...] + lnb_ref[...] + bonus_ref[0]) * g_ref[0]

    pos = pos0 + t * tt + lax.broadcasted_iota(jnp.int32, (tt, LANES), 0)
    out = _dot(y_rwkv.astype(BF16), wout_ref[0:RW, :])
    gd = PW // len(POOL_WINDOWS)
    for gi, win in enumerate(POOL_WINDOWS):
        lanes = slice(gi * gd, (gi + 1) * gd)
        acc = ext_sc[16:16 + tt, lanes]
        for j in range(1, win):
            acc = acc + ext_sc[16 - j:16 - j + tt, lanes]
        cnt = jnp.minimum(pos + 1, win).astype(F32)
        dlt = acc / cnt - pp[:, lanes]
        yp = _dot(dlt.astype(BF16), wpool_ref[gi]) * pscale_ref[:, lanes]
        out = out + _dot(yp.astype(BF16), wout_ref[RW + gi * gd:RW + (gi + 1) * gd, :])
    o_ref[0] = x_ref[0] + gate_ref[0] * out


def rwkv_post(y, g, bonus, pp, pool0, x, gate, p, tt, pos0):
    s, t_len, _ = pp.shape
    const = lambda shape: pl.BlockSpec(shape, lambda i, j: (0,) * len(shape))
    tok = lambda wd: pl.BlockSpec((1, tt, wd), lambda i, j: (i, j, 0))
    gd = PW // len(POOL_WINDOWS)
    return pl.pallas_call(
        functools.partial(_post_kernel, tt=tt, pos0=pos0),
        grid=(s, t_len // tt),
        in_specs=[tok(RW), tok(RW), tok(RW), tok(PW),
                  pl.BlockSpec((1, 16, PW), lambda i, j: (i, 0, 0)), tok(D),
                  pl.BlockSpec((1, 1, D), lambda i, j: (i, 0, 0)),
                  const((1, RW)), const((1, RW)), const((RW, RW)), const((len(POOL_WINDOWS), gd, gd)),
                  const((1, PW)), const((D, D))],
        out_specs=tok(D),
        out_shape=jax.ShapeDtypeStruct((s, t_len, D), F32),
        scratch_shapes=[pltpu.VMEM((16 + tt, PW), F32)],
        compiler_params=_cparams(("arbitrary", "arbitrary")),
        name="rwkv_post",
    )(y, g, bonus, pp, pool0, x, gate, p["ln_w"].reshape(1, RW), p["ln_b"].reshape(1, RW), p["bd"], p["w_pool"],
      p["pool_scale"].reshape(1, PW), p["w_out"])


def _sb_block(q_h, k_bf, v_bf, u_ref, carry_ref, acc_ref, h, valid):
    tk = k_bf.shape[0]
    logit = lax.dot_general(q_h, k_bf, (((1,), (1,)), ((), ())), preferred_element_type=F32)
    sp = jnp.maximum(logit, 0.0) + jnp.log2(1.0 + jnp.exp2(-jnp.abs(logit)))
    log_1m = -sp
    if valid is not None:
        log_1m = jnp.where(valid, log_1m, 0.0)
    hi = log_1m.astype(BF16)
    lo = (log_1m - hi.astype(F32)).astype(BF16)
    u = u_ref[...]
    after = _dot(hi, u) + _dot(lo, u)
    total = jnp.broadcast_to((after + log_1m)[:, 0:1], (after.shape[0], LANES))
    carry = carry_ref[h]
    carry = carry[:, :tk] if tk <= LANES else jnp.concatenate([carry] * (tk // LANES), axis=1)
    wts = jnp.exp2(logit - sp + after + carry)
    if valid is not None:
        wts = jnp.where(valid, wts, 0.0)
    acc_ref[h] += _dot(wts.astype(BF16), v_bf)
    carry_ref[h] += total


def _sb_kernel(*refs, tq, tk, n_cache, tkc, pairs):
    if n_cache:
        q_ref, kn_ref, vn_ref, un_ref, kc_ref, vc_ref, uc_ref, o_ref, carry_ref, acc_ref = refs
    else:
        q_ref, kn_ref, vn_ref, un_ref, o_ref, carry_ref, acc_ref = refs
    qi = pl.program_id(2)
    m = tq // tk
    lane = lax.broadcasted_iota(jnp.int32, (tq, LANES), 1)
    q_pairs = []
    for p in range(pairs):
        q2 = q_ref[0, :, p * LANES:(p + 1) * LANES] * (HD ** -0.5 * LOG2_E)
        q_pairs.append(jnp.concatenate([jnp.where(lane < HD, q2, 0.0), jnp.where(lane >= HD, q2, 0.0)],
                                       axis=0).astype(BF16))
    carry_ref[...] = jnp.zeros_like(carry_ref)
    acc_ref[...] = jnp.zeros_like(acc_ref)

    def visit(k_ref, v_ref, start, rows, u_ref, valid):
        if valid is not None:
            valid = jnp.concatenate([valid, valid], axis=0)
        for p in range(pairs):
            kb = k_ref[0, pl.ds(start, rows), p * LANES:(p + 1) * LANES].astype(BF16)
            vb = v_ref[0, pl.ds(start, rows), p * LANES:(p + 1) * LANES].astype(BF16)
            _sb_block(q_pairs[p], kb, vb, u_ref, carry_ref, acc_ref, p, valid)

    qpos = qi * tq + lax.broadcasted_iota(jnp.int32, (tq, tk), 0)
    col = lax.broadcasted_iota(jnp.int32, (tq, tk), 1)
    for i in range(m):
        start = pl.multiple_of((qi * m + (m - 1 - i)) * tk, tk)
        visit(kn_ref, vn_ref, start, tk, un_ref, start + col < qpos)

    def live():
        return (jnp.max(carry_ref[...]) >= CARRY_FLOOR).astype(jnp.int32)

    def walk(n_tiles, visit_tile):
        def body(state):
            visit_tile(state[0])
            return state[0] + 1, live()

        lax.while_loop(lambda state: jnp.logical_and(state[0] < n_tiles, state[1] > 0), body,
                       (jnp.int32(0), live()))

    walk(qi * m, lambda it: visit(kn_ref, vn_ref, pl.multiple_of((qi * m - 1 - it) * tk, tk), tk, un_ref, None))
    if n_cache:
        walk(n_cache,
             lambda it: visit(kc_ref, vc_ref, pl.multiple_of((n_cache - 1 - it) * tkc, tkc), tkc, uc_ref, None))

    for p in range(pairs):
        o_ref[0, :, p * LANES:(p + 1) * LANES] = jnp.where(lane < HD, acc_ref[p, 0:tq], acc_ref[p, tq:2 * tq])


def _cumsum_matrix(tk):
    return jnp.asarray(np.tril(np.ones((tk, tk)), -1), BF16)


def stick_breaking(q, k, v, ck, cv, tq, tk, tkc, pairs):
    b, t_len, _ = q.shape
    n_cache = 0 if ck is None else ck.shape[1] // tkc
    wd = pairs * LANES
    cols = lambda rows: pl.BlockSpec((1, rows, wd), lambda i, hp, qi: (i, 0, hp))
    const = lambda shape: pl.BlockSpec(shape, lambda i, hp, qi: (0, 0))
    qspec = pl.BlockSpec((1, tq, wd), lambda i, hp, qi: (i, qi, hp))
    in_specs = [qspec, cols(t_len), cols(t_len), const((tk, tk))]
    args = [q, k, v, _cumsum_matrix(tk)]
    if n_cache:
        in_specs += [cols(ck.shape[1]), cols(ck.shape[1]), const((tkc, tkc))]
        args += [ck, cv, _cumsum_matrix(tkc)]
    return pl.pallas_call(
        functools.partial(_sb_kernel, tq=tq, tk=tk, n_cache=n_cache, tkc=tkc, pairs=pairs),
        grid=(b, D // wd, t_len // tq),
        in_specs=in_specs,
        out_specs=qspec,
        out_shape=jax.ShapeDtypeStruct((b, t_len, D), F32),
        scratch_shapes=[pltpu.VMEM((pairs, 2 * tq, LANES), F32), pltpu.VMEM((pairs, 2 * tq, LANES), F32)],
        compiler_params=_cparams(("arbitrary", "arbitrary", "arbitrary")),
        name="stick_breaking",
    )(*args)


def _proj_res_kernel(a_ref, w_ref, x_ref, gate_ref, o_ref):
    y = _dot(a_ref[...].astype(BF16), w_ref[...])
    for q in range(TM // CHUNK):
        rows = slice(q * CHUNK, (q + 1) * CHUNK)
        o_ref[rows, :] = x_ref[rows, :] + gate_ref[0, q:q + 1, :] * y[rows, :]


def proj_residual(a, w_bf, x, gate):
    nt = x.shape[0]
    q = TM // CHUNK
    return pl.pallas_call(
        _proj_res_kernel,
        grid=(nt // TM,),
        in_specs=[pl.BlockSpec((TM, D), lambda i: (i, 0)), pl.BlockSpec((D, D), lambda i: (0, 0)),
                  pl.BlockSpec((TM, D), lambda i: (i, 0)), pl.BlockSpec((1, q, D), lambda i: (i, 0, 0))],
        out_specs=pl.BlockSpec((TM, D), lambda i: (i, 0)),
        out_shape=jax.ShapeDtypeStruct((nt, D), F32),
        compiler_params=_cparams(("arbitrary",)),
        name="proj_residual",
    )(a, w_bf, x, gate)


def _route_kernel(x_ref, g_ref, sh_ref, sc_ref, wr_ref, br_ref, tri_ref, h_o, mi_o, mf_o, cnt_o, carry_sc):
    i = pl.program_id(0)

    @pl.when(i == 0)
    def _():
        carry_sc[...] = jnp.zeros_like(carry_sc)

    _mod_tile(x_ref, g_ref, sh_ref, sc_ref, h_o)
    h = h_o[...]
    h_hi = h.astype(BF16)
    h_lo = (h - h_hi.astype(F32)).astype(BF16)
    w_hi = wr_ref[0]
    w_lo = wr_ref[1]
    lg = _dot(h_hi, w_hi) + _dot(h_hi, w_lo) + _dot(h_lo, w_hi) + br_ref[...]
    lane = lax.broadcasted_iota(jnp.int32, lg.shape, 1)
    lane_f = lane.astype(F32)
    neg = jnp.float32(-3.0e38)
    vals, idxs = [], []
    mask = jnp.zeros(lg.shape, F32)
    for _ in range(TOP_K):
        m = jnp.max(lg, axis=1, keepdims=True)
        idx = jnp.min(jnp.where(lg == m, lane_f, float(LANES)), axis=1, keepdims=True)
        sel = lane_f == idx
        mask = jnp.where(sel, 1.0, mask)
        lg = jnp.where(sel, neg, lg)
        vals.append(m)
        idxs.append(idx)
    es = [jnp.exp(vj - vals[0]) for vj in vals]
    den = es[0] + es[1] + es[2] + es[3]
    rank = _dot(tri_ref[...], mask.astype(BF16)) + carry_sc[...]
    carry_sc[...] += jnp.sum(mask, axis=0, keepdims=True)
    mi = jnp.zeros(lg.shape, F32)
    mf = jnp.zeros(lg.shape, F32)
    for j in range(TOP_K):
        rj = jnp.sum(jnp.where(lane_f == idxs[j], rank, 0.0), axis=1, keepdims=True)
        mi = jnp.where(lane == j, idxs[j], mi)
        mi = jnp.where(lane == TOP_K + j, rj, mi)
        mf = jnp.where(lane == j, es[j] / den, mf)
    mi_o[...] = mi.astype(jnp.int32)
    mf_o[...] = mf
    cnt_o[...] = jnp.broadcast_to(carry_sc[...], cnt_o.shape)


def moe_route(x, g, sh, sc, wr2, br, tri):
    nt = x.shape[0]
    q = TM // CHUNK
    return pl.pallas_call(
        _route_kernel,
        grid=(nt // TM,),
        in_specs=[pl.BlockSpec((TM, D), lambda i: (i, 0)), pl.BlockSpec((1, D), lambda i: (0, 0)),
                  pl.BlockSpec((1, q, D), lambda i: (i, 0, 0)), pl.BlockSpec((1, q, D), lambda i: (i, 0, 0)),
                  pl.BlockSpec((2, D, LANES), lambda i: (0, 0, 0)), pl.BlockSpec((1, LANES), lambda i: (0, 0)),
                  pl.BlockSpec((TM, TM), lambda i: (0, 0))],
        out_specs=[pl.BlockSpec((TM, D), lambda i: (i, 0)), pl.BlockSpec((TM, LANES), lambda i: (i, 0)),
                   pl.BlockSpec((TM, LANES), lambda i: (i, 0)), pl.BlockSpec((8, LANES), lambda i: (0, 0))],
        out_shape=[jax.ShapeDtypeStruct((nt, D), F32), jax.ShapeDtypeStruct((nt, LANES), jnp.int32),
                   jax.ShapeDtypeStruct((nt, LANES), F32), jax.ShapeDtypeStruct((8, LANES), F32)],
        scratch_shapes=[pltpu.VMEM((1, LANES), F32)],
        compiler_params=_cparams(("arbitrary",)),
        name="moe_route",
    )(x, g, sh, sc, wr2, br, tri)


def _dispatch_kernel(dest_ref, h_ref, xs_in, xs_out, sem):
    del xs_in

    def copy(r, j):
        return pltpu.make_async_copy(h_ref.at[pl.ds(r, 1)], xs_out.at[pl.ds(dest_ref[r * TOP_K + j], 1)], sem)

    def issue(r, c):
        for j in range(TOP_K):
            copy(r, j).start()
        return c

    def drain(r, c):
        for j in range(TOP_K):
            copy(r, j).wait()
        return c

    lax.fori_loop(0, TM, issue, 0)
    lax.fori_loop(0, TM, drain, 0)


def moe_dispatch(dest_flat, h, xs_zero):
    nt = h.shape[0]
    return pl.pallas_call(
        _dispatch_kernel,
        grid=(nt // TM,),
        in_specs=[pl.BlockSpec((TM * TOP_K,), lambda i: (i,), memory_space=pltpu.SMEM),
                  pl.BlockSpec((TM, D), lambda i: (i, 0)),
                  pl.BlockSpec(memory_space=pl.ANY)],
        out_specs=pl.BlockSpec(memory_space=pl.ANY),
        out_shape=jax.ShapeDtypeStruct(xs_zero.shape, xs_zero.dtype),
        scratch_shapes=[pltpu.SemaphoreType.DMA(())],
        input_output_aliases={2: 0},
        compiler_params=_cparams(("arbitrary",)),
        name="moe_dispatch",
    )(dest_flat, h, xs_zero)


def _expert_kernel(te_ref, nu_ref, x_ref, wu_ref, bu_ref, wd_ref, bd_ref, o_ref):
    i = pl.program_id(0)

    @pl.when(i < nu_ref[0])
    def _():
        x = x_ref[...].astype(BF16)
        u = _dot(x, wu_ref[0]) + bu_ref[0]
        nxt = pltpu.roll(u, u.shape[1] - 1, axis=1)
        glu = jnp.minimum(u, SWIGLU_LIMIT)
        lin = jnp.clip(nxt, -SWIGLU_LIMIT, SWIGLU_LIMIT)
        act = glu * jax.nn.sigmoid(SWIGLU_ALPHA * glu) * (lin + 1.0)
        lane = lax.broadcasted_iota(jnp.int32, act.shape, 1)
        act = jnp.where(lane % 2 == 0, act, 0.0)
        o_ref[...] = _dot(act.astype(BF16), wd_ref[0]) + bd_ref[0]

    @pl.when(i >= nu_ref[0])
    def _():
        o_ref[...] = jnp.zeros_like(o_ref)


def moe_experts(tile_expert, n_used, xs, wu, bu, wd, bd):
    rows = xs.shape[0]
    xspec = pl.BlockSpec((TME, D), lambda i, te, nu: (i, 0))
    return pl.pallas_call(
        _expert_kernel,
        grid_spec=pltpu.PrefetchScalarGridSpec(
            num_scalar_prefetch=2, grid=(rows // TME,),
            in_specs=[xspec,
                      pl.BlockSpec((1, D, 2 * D), lambda i, te, nu: (te[i], 0, 0)),
                      pl.BlockSpec((1, 1, 2 * D), lambda i, te, nu: (te[i], 0, 0)),
                      pl.BlockSpec((1, 2 * D, D), lambda i, te, nu: (te[i], 0, 0)),
                      pl.BlockSpec((1, 1, D), lambda i, te, nu: (te[i], 0, 0))],
            out_specs=xspec),
        out_shape=jax.ShapeDtypeStruct((rows, D), F32),
        compiler_params=_cparams(("arbitrary",)),
        name="moe_experts",
    )(tile_expert, n_used, xs, wu, bu, wd, bd)


def _combine_kernel(dest_ref, x_ref, mf_ref, gate_ref, ys_ref, o_ref, buf, sem):
    def copy(r, j):
        return pltpu.make_async_copy(ys_ref.at[pl.ds(dest_ref[r * TOP_K + j], 1)], buf.at[j, pl.ds(r, 1)], sem)

    def issue(r, c):
        for j in range(TOP_K):
            copy(r, j).start()
        return c

    def drain(r, c):
        for j in range(TOP_K):
            copy(r, j).wait()
        return c

    lax.fori_loop(0, TM, issue, 0)
    lax.fori_loop(0, TM, drain, 0)
    mf = mf_ref[...]
    y = buf[0] * mf[:, 0:1]
    for j in range(1, TOP_K):
        y = y + buf[j] * mf[:, j:j + 1]
    for q in range(TM // CHUNK):
        rows = slice(q * CHUNK, (q + 1) * CHUNK)
        o_ref[rows, :] = x_ref[rows, :] + gate_ref[0, q:q + 1, :] * y[rows, :]


def moe_combine(dest_flat, x, mf, gate, ys):
    nt = x.shape[0]
    q = TM // CHUNK
    return pl.pallas_call(
        _combine_kernel,
        grid=(nt // TM,),
        in_specs=[pl.BlockSpec((TM * TOP_K,), lambda i: (i,), memory_space=pltpu.SMEM),
                  pl.BlockSpec((TM, D), lambda i: (i, 0)), pl.BlockSpec((TM, LANES), lambda i: (i, 0)),
                  pl.BlockSpec((1, q, D), lambda i: (i, 0, 0)), pl.BlockSpec(memory_space=pl.ANY)],
        out_specs=pl.BlockSpec((TM, D), lambda i: (i, 0)),
        out_shape=jax.ShapeDtypeStruct((nt, D), F32),
        scratch_shapes=[pltpu.VMEM((TOP_K, TM, D), F32), pltpu.SemaphoreType.DMA(())],
        compiler_params=_cparams(("arbitrary",)),
        name="moe_combine",
    )(dest_flat, x, mf, gate, ys)


def moe_layer(x, g, sh, sc, gate, mp, xs_init=None):
    nt = x.shape[0]
    h, mi, mf, cnt = moe_route(x, g, sh, sc, mp["wr2"], mp["br"], mp["tri"])
    counts = cnt[0, :N_EXPERTS].astype(jnp.int32)
    padded = (counts + TME - 1) // TME * TME
    pad_end = jnp.cumsum(padded)
    pad_start = pad_end - padded
    dest = pad_start[mi[:, :TOP_K]] + mi[:, TOP_K:2 * TOP_K]
    dest_flat = dest.reshape(-1).astype(jnp.int32)
    n_tiles = (nt * TOP_K + N_EXPERTS * (TME - 1)) // TME
    tile_start = jnp.arange(n_tiles, dtype=jnp.int32) * TME
    tile_expert = jnp.minimum(jnp.sum(pad_end[None, :] <= tile_start[:, None], axis=1), N_EXPERTS - 1).astype(jnp.int32)
    n_used = (pad_end[-1:] // TME).astype(jnp.int32)
    if xs_init is None:
        xs_init = jnp.zeros((n_tiles * TME, D), F32)
    xs = moe_dispatch(dest_flat, h, xs_init)
    ys = moe_experts(tile_expert, n_used, xs, mp["wu"], mp["bu"], mp["wd"], mp["bd"])
    return moe_combine(dest_flat, x, mf, gate, ys), ys


def _final_kernel(x_ref, g_ref, o_ref):
    x = x_ref[...]
    ms = jnp.mean(x * x, axis=-1, keepdims=True)
    o_ref[...] = x * lax.rsqrt(ms + EPS) * g_ref[...]


def final_norm(x, g):
    nt = x.shape[0]
    return pl.pallas_call(
        _final_kernel,
        grid=(nt // TM,),
        in_specs=[pl.BlockSpec((TM, D), lambda i: (i, 0)), pl.BlockSpec((1, D), lambda i: (0, 0))],
        out_specs=pl.BlockSpec((TM, D), lambda i: (i, 0)),
        out_shape=jax.ShapeDtypeStruct((nt, D), F32),
        compiler_params=_cparams(("arbitrary",)),
        name="final_norm",
    )(x, g.reshape(1, D))


def rwkv_pool_layer(x, mods, g, lp, groups, state):
    sh, sc, gate_rows = mods
    ps, pp = normmm(x, g, sh, sc, lp["w_in"], (SHIFT_W, PW))
    outs, states = [], []
    for (r0, s, t_len, tt, pos0, group), (wkv0, shift0, pool0) in zip(groups, state):
        n = s * t_len
        ps_g = ps[r0:r0 + n].reshape(s, t_len, SHIFT_W)
        pp_g = pp[r0:r0 + n].reshape(s, t_len, PW)
        r, w, k, a, b, vt, gt, bonus = rwkv_prep(ps_g, shift0, lp, tt)
        flat = lambda z: z.reshape(s * HEADS, t_len, HD)
        s_new, y = wkv_scan(wkv0.reshape(s * HEADS, HD, HD), flat(r), flat(w), flat(k), flat(a), flat(b),
                            vt.reshape(s * HEADS, HD, vt.shape[2]), t_len, group)
        pool_pad = jnp.concatenate([jnp.zeros((s, 1, PW), F32), pool0], axis=1)
        x_g = x[r0:r0 + n].reshape(s, t_len, D)
        out = rwkv_post(y, gt, bonus, pp_g, pool_pad, x_g,
                        gate_rows[r0 // CHUNK:(r0 + n) // CHUNK:t_len // CHUNK].reshape(s, 1, D), lp, tt, pos0)
        outs.append(out.reshape(n, D))
        states.append((s_new.reshape(s, HEADS, HD, HD), ps_g[:, -1], pp_g[:, t_len - POOL_HIST:]))
    return jnp.concatenate(outs, axis=0), states


def attention_layer(x, mods, g, lp, groups, caches):
    sh, sc, gate = mods
    q, k, v = normmm(x, g, sh, sc, lp["w_qkv"], (D, D, D))
    outs, kvs = [], []
    for (r0, s, t_len, tq, tk, tkc, pairs), cache in zip(groups, caches):
        n = s * t_len
        seq = lambda z: z[r0:r0 + n].reshape(s, t_len, D)
        ck, cv = cache
        outs.append(stick_breaking(seq(q), seq(k), seq(v), ck, cv, tq, tk, tkc, pairs).reshape(n, D))
        kvs.append((seq(k).reshape(s, t_len, D // HD, HD), seq(v).reshape(s, t_len, D // HD, HD)))
    o = jnp.concatenate(outs, axis=0)
    return proj_residual(o, lp["w_out"], x, gate), kvs


def kernel(x_prompt, x_sample, c_prompt, c_sample, state_wkv, state_shift, state_pool, cache_k, cache_v, ada_w, ada_b, norm_mix_g, norm_ffn_g, final_g, a_w_in, a_mu, a_w0, a_w_decay, a_a0, a_w_iclr, a_w_gate, a_k_k, a_k_a, a_r_k, a_ln_w, a_ln_b, b_w_pool, b_pool_scale, ab_w_out, c_w_qkv, c_w_out, moe_w_router, moe_b_router, moe_w_up, moe_b_up, moe_w_down, moe_b_down):
    bp, tp, _ = x_prompt.shape
    bs, ts, _ = x_sample.shape
    past = cache_k.shape[2]
    n_p, n_s = bp * tp, bs * ts
    nt = n_p + n_s
    x = jnp.concatenate([x_prompt.reshape(n_p, D), x_sample.reshape(n_s, D)], axis=0)

    mod = ada_mod(jnp.concatenate([c_prompt, c_sample], axis=0), ada_w, ada_b)
    seq_of_chunk = np.concatenate([np.repeat(np.arange(bp), tp // CHUNK), bp + np.repeat(np.arange(bs), ts // CHUNK)])
    q = TM // CHUNK

    def mod_rows(layer, idx):
        rows = mod[layer, :, idx * D:(idx + 1) * D][seq_of_chunk]
        return rows, rows.reshape(nt // TM, q, D)

    zeros = lambda *shape: jnp.zeros(shape, F32)
    bd = jnp.asarray(np.kron(np.eye(HEADS), np.ones((HD, HD))), BF16)
    tri = jnp.asarray(np.tril(np.ones((TM, TM)), -1), BF16)

    def moe_params(layer):
        wr = jnp.pad(moe_w_router[layer], ((0, 0), (0, LANES - N_EXPERTS)))
        wr_hi = wr.astype(BF16)
        wr_lo = (wr - wr_hi.astype(F32)).astype(BF16)
        br = jnp.concatenate([moe_b_router[layer], jnp.full((LANES - N_EXPERTS,), -1e30, F32)]).reshape(1, LANES)
        wd = moe_w_down[layer].astype(BF16)
        wd = jnp.stack([wd, jnp.zeros_like(wd)], axis=2).reshape(N_EXPERTS, 2 * D, D)
        return dict(wr2=jnp.stack([wr_hi, wr_lo]), br=br, tri=tri,
                    wu=moe_w_up[layer].astype(BF16), bu=moe_b_up[layer][:, None, :],
                    wd=wd, bd=moe_b_down[layer][:, None, :])

    w_lora = jnp.zeros((128, 2 * RW), F32).at[:64, :RW].set(a_w_decay[0]).at[64:, RW:].set(a_w_iclr[0])
    lp0 = dict(w_in=a_w_in[0].astype(BF16), mu=a_mu[0], w0=a_w0[0], w_lora=w_lora.astype(BF16), a0=a_a0[0],
               w_gate=a_w_gate[0].astype(BF16), k_k=a_k_k[0], k_a=a_k_a[0], r_k=a_r_k[0], bd=bd,
               ln_w=a_ln_w[0], ln_b=a_ln_b[0], w_pool=b_w_pool[0].astype(BF16), pool_scale=b_pool_scale[0],
               w_out=ab_w_out[0].astype(BF16))
    lp1 = dict(w_qkv=c_w_qkv[0].astype(BF16), w_out=c_w_out[0].astype(BF16))

    sh1, sc1, g1 = (mod_rows(0, i) for i in (0, 1, 2))
    groups0 = [(0, bp, tp, 256, 0, 16), (n_p, bs, ts, ts, past, 16)]
    state0 = [(zeros(bp, HEADS, HD, HD), zeros(bp, SHIFT_W), zeros(bp, POOL_HIST, PW)),
              (state_wkv[0], state_shift[0], state_pool[0])]
    x, st0 = rwkv_pool_layer(x, (sh1[1], sc1[1], g1[0]), norm_mix_g[0].reshape(1, D), lp0, groups0, state0)
    sh2, sc2, g2 = (mod_rows(0, i) for i in (3, 4, 5))
    x, ys0 = moe_layer(x, norm_ffn_g[0].reshape(1, D), sh2[1], sc2[1], g2[1], moe_params(0))

    sh1, sc1, g1 = (mod_rows(1, i) for i in (0, 1, 2))
    groups1 = [(0, bp, tp, 256, 256, 256, 1), (n_p, bs, ts, ts, ts, 256, 4)]
    caches = [(None, None), (cache_k[0].reshape(bs, past, D), cache_v[0].reshape(bs, past, D))]
    x, kv1 = attention_layer(x, (sh1[1], sc1[1], g1[1]), norm_mix_g[1].reshape(1, D), lp1, groups1, caches)
    sh2, sc2, g2 = (mod_rows(1, i) for i in (3, 4, 5))
    x, _ = moe_layer(x, norm_ffn_g[1].reshape(1, D), sh2[1], sc2[1], g2[1], moe_params(1), xs_init=ys0)

    y = final_norm(x, final_g)
    lead = lambda z: z[None]
    (p_wkv, p_shift, p_pool), (s_wkv, s_shift, s_pool) = st0
    (p_k, p_v), (s_k, s_v) = kv1
    return (y[:n_p].reshape(bp, tp, D), y[n_p:].reshape(bs, ts, D),
            lead(p_wkv), lead(p_shift), lead(p_pool), lead(p_k), lead(p_v),
            lead(s_wkv), lead(s_shift), lead(s_pool), lead(s_k), lead(s_v))
```
